```python
import math
import jax, jax.numpy as jnp
from jax import lax
import numpy as np

D_MODEL = 2048
BATCH = 8
SEQ = 2048
DEPTH = 2
DEC_BATCH = 128
DEC_SEQ = 4
PAST_LEN = 16384
PAGE_SIZE = 128

PLE_DIM = 256
N_EVEN = (DEPTH + 1) // 2
N_ODD = DEPTH // 2
NORM_EPS = 1e-6
SC_WIDTH = D_MODEL
SC_CONV = 3
DN_HEADS = D_MODEL // 128
DN_DK = 128
DN_DV = 128
DN_QK = DN_HEADS * DN_DK
DN_VW = DN_HEADS * DN_DV
DN_CONV = 4
DN_CHUNK = 64
MLA_HEADS = D_MODEL // 128
Q_LORA = 512
KV_LORA = 512
NOPE = 128
ROPE = 64
V_DIM = 128
ROPE_THETA = 10000.0
Q_BLOCK = 128
MLA_SCALE = (NOPE + ROPE) ** -0.5
EVEN_IN = 4 * SC_WIDTH + 2 * DN_QK + 2 * DN_VW + 2 * DN_HEADS
EVEN_MIX = SC_WIDTH + DN_VW
MLA_IN = Q_LORA + KV_LORA + ROPE + MLA_HEADS * V_DIM

kernel_name = 'hybrid_shortconv_gdn_mla_decode_step'


def _rms(x, g):
    xf = x.astype(jnp.float32)
    y = xf * lax.rsqrt(jnp.mean(xf * xf, axis=-1, keepdims=True) + NORM_EPS)
    return (y * g.astype(jnp.float32)).astype(x.dtype)


def _l2n(x):
    xf = x.astype(jnp.float32)
    return xf * lax.rsqrt(jnp.sum(xf * xf, axis=-1, keepdims=True) + NORM_EPS)


def _causal_conv(x, buf, w):
    width, seq = w.shape[0], x.shape[1]
    xp = jnp.concatenate([buf.astype(x.dtype), x], axis=1)
    y = sum(xp[:, j:j + seq] * w[j] for j in range(width))
    return y, xp[:, seq:]


def _rope(x, pos):
    half = x.shape[-1] // 2
    inv = ROPE_THETA ** (-jnp.arange(half, dtype=jnp.float32) / half)
    ang = pos.astype(jnp.float32)[:, None] * inv
    ang = ang.reshape(ang.shape[:1] + (1,) * (x.ndim - 3) + ang.shape[1:])
    cos, sin = jnp.cos(ang), jnp.sin(ang)
    xf = x.astype(jnp.float32)
    x1, x2 = xf[..., :half], xf[..., half:]
    return jnp.concatenate([x1 * cos - x2 * sin, x2 * cos + x1 * sin], axis=-1).astype(x.dtype)


def _gated_delta(q, k, v, g, beta, s0):
    bsz, seq, nh, _ = q.shape
    dv = v.shape[-1]
    c = min(DN_CHUNK, seq)
    n = -(-seq // c)
    pad = n * c - seq

    def blocks(t):
        t = t.astype(jnp.float32)
        t = jnp.pad(t, [(0, 0), (0, pad)] + [(0, 0)] * (t.ndim - 2))
        t = t.reshape((bsz, n, c) + t.shape[2:])
        return jnp.swapaxes(jnp.moveaxis(t, 1, 0), 2, 3)

    qc, kc, vc, gc, bc = blocks(q), blocks(k), blocks(v), blocks(g), blocks(beta)
    gc = jnp.cumsum(gc, axis=-1)
    idx = jnp.arange(c)
    lower = idx[:, None] >= idx[None, :]
    strict = idx[:, None] > idx[None, :]
    decay = jnp.exp(jnp.where(lower, gc[..., :, None] - gc[..., None, :], -jnp.inf))
    kb = kc * bc[..., None]
    a_kk = jnp.where(strict, jnp.einsum('nbhid,nbhjd->nbhij', kb, kc) * decay, 0.0)
    rhs = jnp.concatenate([vc * bc[..., None], kb * jnp.exp(gc)[..., None]], axis=-1)
    sol = lax.linalg.triangular_solve(a_kk, rhs, left_side=True, lower=True, unit_diagonal=True)
    u, w = sol[..., :dv], sol[..., dv:]
    a_qk = jnp.einsum('nbhid,nbhjd->nbhij', qc, kc) * decay

    def step(s, xs):
        q_i, k_i, u_i, w_i, g_i, a_i = xs
        v_new = u_i - jnp.einsum('bhcd,bhde->bhce', w_i, s)
        o_i = (jnp.einsum('bhcd,bhde->bhce', q_i * jnp.exp(g_i)[..., None], s)
               + jnp.einsum('bhij,bhje->bhie', a_i, v_new))
        g_last = g_i[..., -1]
        s = (s * jnp.exp(g_last)[..., None, None]
             + jnp.einsum('bhcd,bhce->bhde', k_i * jnp.exp(g_last[..., None] - g_i)[..., None], v_new))
        return s, o_i

    s_fin, o = lax.scan(step, s0.astype(jnp.float32), (qc, kc, u, w, gc, a_qk))
    o = jnp.moveaxis(jnp.swapaxes(o, 2, 3), 0, 1).reshape(bsz, n * c, nh, dv)[:, :seq]
    return o, s_fin.astype(s0.dtype)


def _even_mixer(xn, conv_a0, conv_qkv0, s0, w_in, w_out, conv_a_w, conv_qkv_w, a_log, dt_bias, dn_norm_g):
    bsz, seq, _ = xn.shape
    proj = jnp.einsum('bld,de->ble', xn, w_in)
    cuts = [int(t) for t in np.cumsum([SC_WIDTH] * 4 + [DN_QK, DN_QK, DN_VW, DN_VW, DN_HEADS])]
    a_c, a_h, a_b, a_z, q, k, v, z, b_logit, alpha = jnp.split(proj, cuts, axis=-1)
    u, conv_a1 = _causal_conv(a_c * a_h, conv_a0, conv_a_w)
    y_a = a_b * u * jax.nn.silu(a_z)
    qkv, conv_qkv1 = _causal_conv(jnp.concatenate([q, k, v], axis=-1), conv_qkv0, conv_qkv_w)
    q, k, v = jnp.split(jax.nn.silu(qkv), [DN_QK, 2 * DN_QK], axis=-1)
    q = _l2n(q.reshape(bsz, seq, DN_HEADS, DN_DK)) * (DN_DK ** -0.5)
    k = _l2n(k.reshape(bsz, seq, DN_HEADS, DN_DK))
    v = v.reshape(bsz, seq, DN_HEADS, DN_DV)
    beta = jax.nn.sigmoid(b_logit.astype(jnp.float32))
    g = -jnp.exp(a_log.astype(jnp.float32)) * jax.nn.softplus(alpha.astype(jnp.float32) + dt_bias.astype(jnp.float32))
    o, s1 = _gated_delta(q, k, v, g, beta, s0)
    o = _rms(o, dn_norm_g) * jax.nn.silu(z.reshape(bsz, seq, DN_HEADS, DN_DV).astype(jnp.float32))
    y_b = o.reshape(bsz, seq, DN_VW).astype(xn.dtype)
    y = jnp.einsum('ble,ed->bld', jnp.concatenate([y_a, y_b], axis=-1), w_out)
    return y, conv_a1, conv_qkv1, s1


def _mla_project(xn, pos, w_in, q_norm_g, kv_norm_g, w_uq):
    bsz, seq, _ = xn.shape
    proj = jnp.einsum('bld,de->ble', xn, w_in)
    c_q, c_kv, k_pe, gate = jnp.split(proj, [Q_LORA, Q_LORA + KV_LORA, Q_LORA + KV_LORA + ROPE], axis=-1)
    q = jnp.einsum('blr,re->ble', _rms(c_q, q_norm_g), w_uq).reshape(bsz, seq, MLA_HEADS, NOPE + ROPE)
    return q[..., :NOPE], _rope(q[..., NOPE:], pos), _rms(c_kv, kv_norm_g), _rope(k_pe, pos), gate


def _mla_attend_prompt(q_nope, q_pe, ckv, kpe, w_uk, w_uv):
    bsz, seq, nh, _ = q_nope.shape
    k_nope = jnp.einsum('bkc,chn->bkhn', ckv, w_uk)
    v = jnp.einsum('bkc,chv->bkhv', ckv, w_uv)
    qb = min(Q_BLOCK, seq)
    nblk = seq // qb
    qn = jnp.moveaxis(q_nope.reshape(bsz, nblk, qb, nh, NOPE), 1, 0)
    qp = jnp.moveaxis(q_pe.reshape(bsz, nblk, qb, nh, ROPE), 1, 0)
    k_pos = jnp.arange(seq)

    def block(args):
        qn_b, qp_b, i = args
        s = (jnp.einsum('bqhn,bkhn->bhqk', qn_b, k_nope)
             + jnp.einsum('bqhr,bkr->bhqk', qp_b, kpe)).astype(jnp.float32) * MLA_SCALE
        q_pos = i * qb + jnp.arange(qb)
        s = jnp.where(k_pos[None, :] <= q_pos[:, None], s, -jnp.inf)
        p = jax.nn.softmax(s, axis=-1).astype(v.dtype)
        return jnp.einsum('bhqk,bkhv->bqhv', p, v)

    o = lax.map(block, (qn, qp, jnp.arange(nblk)))
    return jnp.moveaxis(o, 0, 1).reshape(bsz, seq, nh, V_DIM)


def _mla_attend_sample(q_nope, q_pe, ckv, kpe, cache_ckv, cache_kpe, page_table, layer, w_uk, w_uv):
    q_lat = jnp.einsum('bqhn,chn->bhqc', q_nope, w_uk)
    q_r = jnp.swapaxes(q_pe, 1, 2)
    seq = q_nope.shape[1]

    def scores(c, r):
        return (jnp.einsum('bhqc,bkc->bhqk', q_lat, c)
                + jnp.einsum('bhqr,bkr->bhqk', q_r, r)).astype(jnp.float32) * MLA_SCALE

    idx = jnp.arange(seq)
    s = jnp.where(idx[None, :] <= idx[:, None], scores(ckv, kpe), -jnp.inf)
    m = jnp.max(s, axis=-1)
    e = jnp.exp(s - m[..., None])
    carry = (m, jnp.sum(e, axis=-1), jnp.einsum('bhqk,bkc->bhqc', e, ckv.astype(jnp.float32)))

    def page_step(carry, pages):
        m, l, acc = carry
        c = cache_ckv[layer, pages]
        r = cache_kpe[layer, pages]
        s = scores(c, r)
        m_new = jnp.maximum(m, jnp.max(s, axis=-1))
        e = jnp.exp(s - m_new[..., None])
        corr = jnp.exp(m - m_new)
        acc = acc * corr[..., None] + jnp.einsum('bhqk,bkc->bhqc', e, c.astype(jnp.float32))
        return (m_new, l * corr + jnp.sum(e, axis=-1), acc), None

    (m, l, acc), _ = lax.scan(page_step, carry, page_table.T)
    o_lat = (acc / l[..., None]).astype(q_nope.dtype)
    return jnp.einsum('bhqc,chv->bqhv', o_lat, w_uv)


def _layer_stack(x, p, pos, conv_a0, conv_qkv0, delta0, past, norm_g, ple_norm_g, w_ple_gate, w_ple_proj,
                 final_norm_g, w_in_even, w_out_even, conv_a_w, conv_qkv_w, dn_a_log, dn_dt_bias, dn_norm_g,
                 w_in_mla, mla_q_norm_g, mla_kv_norm_g, w_uq, w_uk, w_uv, w_o_mla):
    h = x
    bsz, seq = x.shape[:2]
    conv_a1, conv_qkv1, delta1, ckv1, kpe1 = [], [], [], [], []
    for i in range(DEPTH):
        j = i // 2
        xn = _rms(h, norm_g[i])
        if i % 2 == 0:
            mix, ca, cq, sd = _even_mixer(xn, conv_a0[j], conv_qkv0[j], delta0[j], w_in_even[j], w_out_even[j],
                                          conv_a_w[j], conv_qkv_w[j], dn_a_log[j], dn_dt_bias[j], dn_norm_g[j])
            conv_a1.append(ca)
            conv_qkv1.append(cq)
            delta1.append(sd)
        else:
            q_nope, q_pe, ckv, kpe, gate = _mla_project(xn, pos, w_in_mla[j], mla_q_norm_g[j], mla_kv_norm_g[j], w_uq[j])
            if past is None:
                o = _mla_attend_prompt(q_nope, q_pe, ckv, kpe, w_uk[j], w_uv[j])
            else:
                cache_ckv, cache_kpe, page_table = past
                o = _mla_attend_sample(q_nope, q_pe, ckv, kpe, cache_ckv, cache_kpe, page_table, j, w_uk[j], w_uv[j])
            mix = jnp.einsum('ble,ed->bld', o.reshape(bsz, seq, MLA_HEADS * V_DIM) * jax.nn.silu(gate), w_o_mla[j])
            ckv1.append(ckv)
            kpe1.append(kpe)
        h = h + mix
        ple_gate = jax.nn.sigmoid(jnp.einsum('bld,de->ble', _rms(h, ple_norm_g[i]), w_ple_gate[i]).astype(jnp.float32))
        h = h + (ple_gate * jnp.einsum('blp,pd->bld', p[i], w_ple_proj[i]).astype(jnp.float32)).astype(h.dtype)
    return (_rms(h, final_norm_g), jnp.stack(conv_a1), jnp.stack(conv_qkv1), jnp.stack(delta1),
            jnp.stack(ckv1), jnp.stack(kpe1))


def setup_inputs(seed: int = 0) -> dict:
    key = jax.random.key(seed)
    keys = iter(jax.random.split(key, 40))
    f32 = jnp.float32
    n_pages = PAST_LEN // PAGE_SIZE
    n_phys = (DEC_BATCH * n_pages * 5) // 4
    conv_ch = 2 * DN_QK + DN_VW

    def nrm(shape, scale=1.0):
        return jax.random.normal(next(keys), shape, f32) * scale

    def gain(shape):
        return 1.0 + nrm(shape, 0.02)

    page_table = jax.random.permutation(next(keys), n_phys)[:DEC_BATCH * n_pages].reshape(DEC_BATCH, n_pages).astype(jnp.int32)
    dt = jnp.exp(jax.random.uniform(next(keys), (N_EVEN, DN_HEADS), f32, math.log(1e-3), math.log(1e-1)))
    dt_bias = dt + jnp.log(-jnp.expm1(-dt))
    a_log = jnp.log(jax.random.uniform(next(keys), (N_EVEN, DN_HEADS), f32, 1.0, 16.0))
    return {
        'x_prompt': nrm((BATCH, SEQ, D_MODEL)),
        'x_sample': nrm((DEC_BATCH, DEC_SEQ, D_MODEL)),
        'cache_ckv': nrm((N_ODD, n_phys, PAGE_SIZE, KV_LORA)),
        'cache_kpe': nrm((N_ODD, n_phys, PAGE_SIZE, ROPE)),
        'state_conv_a': nrm((N_EVEN, DEC_BATCH, SC_CONV - 1, SC_WIDTH)),
        'state_conv_qkv': nrm((N_EVEN, DEC_BATCH, DN_CONV - 1, conv_ch)),
        'state_delta': nrm((N_EVEN, DEC_BATCH, DN_HEADS, DN_DK, DN_DV), 0.1),
        'page_table': page_table,
        'p_prompt': nrm((DEPTH, BATCH, SEQ, PLE_DIM)),
        'p_sample': nrm((DEPTH, DEC_BATCH, DEC_SEQ, PLE_DIM)),
        'norm_g': gain((DEPTH, D_MODEL)),
        'ple_norm_g': gain((DEPTH, D_MODEL)),
        'w_ple_gate': nrm((DEPTH, D_MODEL, D_MODEL), D_MODEL ** -0.5),
        'w_ple_proj': nrm((DEPTH, PLE_DIM, D_MODEL), PLE_DIM ** -0.5),
        'final_norm_g': gain((D_MODEL,)),
        'w_in_even': nrm((N_EVEN, D_MODEL, EVEN_IN), D_MODEL ** -0.5),
        'w_out_even': nrm((N_EVEN, EVEN_MIX, D_MODEL), EVEN_MIX ** -0.5),
        'conv_a_w': nrm((N_EVEN, SC_CONV, SC_WIDTH), SC_CONV ** -0.5),
        'conv_qkv_w': nrm((N_EVEN, DN_CONV, conv_ch), DN_CONV ** -0.5),
        'dn_a_log': a_log,
        'dn_dt_bias': dt_bias,
        'dn_norm_g': gain((N_EVEN, DN_DV)),
        'w_in_mla': nrm((N_ODD, D_MODEL, MLA_IN), D_MODEL ** -0.5),
        'mla_q_norm_g': gain((N_ODD, Q_LORA)),
        'mla_kv_norm_g': gain((N_ODD, KV_LORA)),
        'w_uq': nrm((N_ODD, Q_LORA, MLA_HEADS * (NOPE + ROPE)), Q_LORA ** -0.5),
        'w_uk': nrm((N_ODD, KV_LORA, MLA_HEADS, NOPE), KV_LORA ** -0.5),
        'w_uv': nrm((N_ODD, KV_LORA, MLA_HEADS, V_DIM), KV_LORA ** -0.5),
        'w_o_mla': nrm((N_ODD, MLA_HEADS * V_DIM, D_MODEL), (MLA_HEADS * V_DIM) ** -0.5),
    }


def reference(x_prompt, x_sample, cache_ckv, cache_kpe, state_conv_a, state_conv_qkv, state_delta, page_table,
              p_prompt, p_sample, norm_g, ple_norm_g, w_ple_gate, w_ple_proj, final_norm_g, w_in_even, w_out_even,
              conv_a_w, conv_qkv_w, dn_a_log, dn_dt_bias, dn_norm_g, w_in_mla, mla_q_norm_g, mla_kv_norm_g,
              w_uq, w_uk, w_uv, w_o_mla):
    bsz, seq = x_prompt.shape[:2]
    past_len = page_table.shape[1] * PAGE_SIZE
    pos_prompt = jnp.arange(seq)
    pos_sample = past_len + jnp.arange(x_sample.shape[1])
    zero_conv_a = jnp.zeros((N_EVEN, bsz, SC_CONV - 1, SC_WIDTH), x_prompt.dtype)
    zero_conv_qkv = jnp.zeros((N_EVEN, bsz, DN_CONV - 1, 2 * DN_QK + DN_VW), x_prompt.dtype)
    zero_delta = jnp.zeros((N_EVEN, bsz, DN_HEADS, DN_DK, DN_DV), jnp.float32)
    y_prompt, p_conv_a, p_conv_qkv, p_delta, p_ckv, p_kpe = _layer_stack(
        x_prompt, p_prompt, pos_prompt, zero_conv_a, zero_conv_qkv, zero_delta, None,
        norm_g, ple_norm_g, w_ple_gate, w_ple_proj, final_norm_g, w_in_even, w_out_even, conv_a_w, conv_qkv_w,
        dn_a_log, dn_dt_bias, dn_norm_g, w_in_mla, mla_q_norm_g, mla_kv_norm_g, w_uq, w_uk, w_uv, w_o_mla)
    y_sample, s_conv_a, s_conv_qkv, s_delta, s_ckv, s_kpe = _layer_stack(
        x_sample, p_sample, pos_sample, state_conv_a, state_conv_qkv, state_delta, (cache_ckv, cache_kpe, page_table),
        norm_g, ple_norm_g, w_ple_gate, w_ple_proj, final_norm_g, w_in_even, w_out_even, conv_a_w, conv_qkv_w,
        dn_a_log, dn_dt_bias, dn_norm_g, w_in_mla, mla_q_norm_g, mla_kv_norm_g, w_uq, w_uk, w_uv, w_o_mla)
    return (y_prompt, y_sample, p_conv_a, p_conv_qkv, p_delta, p_ckv, p_kpe,
            s_conv_a, s_conv_qkv, s_delta, s_ckv, s_kpe)
```

```python
import functools
import math

import jax
import jax.numpy as jnp
from jax import lax
from jax.experimental import pallas as pl
from jax.experimental.pallas import tpu as pltpu

F32 = jnp.float32
BF16 = jnp.bfloat16

NORM_EPS = 1e-6
D_MODEL = 2048
HEADS = 16
HEAD_DIM = 128
DN_CHUNK = 64
DN_CONV = 4
SC_CONV = 3
Q_LORA = 512
KV_LORA = 512
NOPE = 128
ROPE = 64
V_DIM = 128
PLE_DIM = 256
ROPE_THETA = 10000.0
PAGE_SIZE = 128
MLA_SCALE = (NOPE + ROPE) ** -0.5
QK_PAD = 256
MLA_IN_PAD = 3584
GATE_COL0 = Q_LORA + KV_LORA
PAIR_COL0 = GATE_COL0 + HEADS * V_DIM
HEAD_GROUP = 4
PAGES_PER_STEP = 8

V7X_VMEM_LIMIT_BYTES = 56 * 1024 * 1024


def _cparams(*sem):
    return pltpu.CompilerParams(dimension_semantics=sem, vmem_limit_bytes=V7X_VMEM_LIMIT_BYTES)


def _sigmoid(x):
    return 1.0 / (1.0 + jnp.exp(-x))


def _silu(x):
    return x * _sigmoid(x)


def _rms_rows(x, g):
    return x * lax.rsqrt(jnp.mean(x * x, axis=-1, keepdims=True) + NORM_EPS) * g


def _dot(a, b):
    return jnp.dot(a, b, preferred_element_type=F32)


def _dot_nt(a, b):
    return lax.dot_general(a, b, (((1,), (1,)), ((), ())), preferred_element_type=F32)


def _dot_tn(a, b):
    return lax.dot_general(a, b, (((0,), (0,)), ((), ())), preferred_element_type=F32)


def _rms_mm_body(x_ref, g_ref, w_ref, o_ref, xn_ref):
    @pl.when(pl.program_id(1) == 0)
    def _():
        xn_ref[...] = _rms_rows(x_ref[...], g_ref[...]).astype(BF16)

    o_ref[...] = _dot(xn_ref[...], w_ref[...]).astype(o_ref.dtype)


def _rms_matmul(x, g, w, out_dtype, tm, tn, name):
    t, k = x.shape
    n = w.shape[1]
    return pl.pallas_call(
        _rms_mm_body,
        grid=(t // tm, n // tn),
        in_specs=[pl.BlockSpec((tm, k), lambda i, j: (i, 0)),
                  pl.BlockSpec((1, k), lambda i, j: (0, 0)),
                  pl.BlockSpec((k, tn), lambda i, j: (0, j))],
        out_specs=pl.BlockSpec((tm, tn), lambda i, j: (i, j)),
        out_shape=jax.ShapeDtypeStruct((t, n), out_dtype),
        scratch_shapes=[pltpu.VMEM((tm, k), BF16)],
        compiler_params=_cparams("parallel", "arbitrary"),
        name=name,
    )(x, g, w)


def _rms_mm_gates_body(x_ref, g_ref, w_ref, alog_ref, dtb_ref, o_ref):
    xn = _rms_rows(x_ref[...], g_ref[...]).astype(BF16)
    y = _dot(xn, w_ref[...])
    lane = lax.broadcasted_iota(jnp.int32, y.shape, 1)
    beta = _sigmoid(y)
    t = y + dtb_ref[...]
    softplus = jnp.maximum(t, 0.0) + jnp.log1p(jnp.exp(-jnp.abs(t)))
    decay = -jnp.exp(alog_ref[...]) * softplus
    o_ref[...] = jnp.where(lane < HEADS, beta, decay)


def _rms_matmul_gates(x, g, w, alog_row, dtb_row, tm):
    t, k = x.shape
    return pl.pallas_call(
        _rms_mm_gates_body,
        grid=(t // tm,),
        in_specs=[pl.BlockSpec((tm, k), lambda i: (i, 0)),
                  pl.BlockSpec((1, k), lambda i: (0, 0)),
                  pl.BlockSpec((k, 128), lambda i: (0, 0)),
                  pl.BlockSpec((1, 128), lambda i: (0, 0)),
                  pl.BlockSpec((1, 128), lambda i: (0, 0))],
        out_specs=pl.BlockSpec((tm, 128), lambda i: (i, 0)),
        out_shape=jax.ShapeDtypeStruct((t, 128), F32),
        compiler_params=_cparams("parallel"),
        name="dn_gates",
    )(x, g, w, alog_row, dtb_row)


def _mm_res_body(*refs, n_in):
    xs, ws, res_ref, o_ref = refs[:n_in], refs[n_in:2 * n_in], refs[2 * n_in], refs[2 * n_in + 1]
    acc = res_ref[...]
    for x_ref, w_ref in zip(xs, ws):
        acc = acc + _dot(x_ref[...], w_ref[...])
    o_ref[...] = acc


def _matmul_residual(xs, ws, res, tm, tn, name):
    t, n = res.shape
    n_in = len(xs)
    in_specs = ([pl.BlockSpec((tm, x.shape[1]), lambda i, j: (i, 0)) for x in xs]
                + [pl.BlockSpec((w.shape[0], tn), lambda i, j: (0, j)) for w in ws]
                + [pl.BlockSpec((tm, tn), lambda i, j: (i, j))])
    return pl.pallas_call(
        functools.partial(_mm_res_body, n_in=n_in),
        grid=(t // tm, n // tn),
        in_specs=in_specs,
        out_specs=pl.BlockSpec((tm, tn), lambda i, j: (i, j)),
        out_shape=jax.ShapeDtypeStruct((t, n), F32),
        compiler_params=_cparams("parallel", "arbitrary"),
        name=name,
    )(*xs, *ws, res)


def _ple_body(h_ref, g_ref, wg_ref, p_ref, wp_ref, fg_ref, o_ref, *, final_norm):
    h = h_ref[...]
    xn = _rms_rows(h, g_ref[...]).astype(BF16)
    gate = _sigmoid(_dot(xn, wg_ref[...]))
    h2 = h + gate * _dot(p_ref[...].astype(BF16), wp_ref[...])
    if final_norm:
        h2 = _rms_rows(h2, fg_ref[...])
    o_ref[...] = h2


def _ple(h, g, w_gate, p, w_proj, final_g, final_norm, tm, name):
    t, d = h.shape
    pd = p.shape[1]
    return pl.pallas_call(
        functools.partial(_ple_body, final_norm=final_norm),
        grid=(t // tm,),
        in_specs=[pl.BlockSpec((tm, d), lambda i: (i, 0)),
                  pl.BlockSpec((1, d), lambda i: (0, 0)),
                  pl.BlockSpec((d, d), lambda i: (0, 0)),
                  pl.BlockSpec((tm, pd), lambda i: (i, 0)),
                  pl.BlockSpec((pd, d), lambda i: (0, 0)),
                  pl.BlockSpec((1, d), lambda i: (0, 0))],
        out_specs=pl.BlockSpec((tm, d), lambda i: (i, 0)),
        out_shape=jax.ShapeDtypeStruct((t, d), F32),
        compiler_params=_cparams("parallel"),
        name=name,
    )(h, g, w_gate, p, w_proj, final_g)


def _mixer_a_body(ac_ref, ah_ref, ab_ref, az_ref, w_ref, st0_ref, y_ref, st_ref, carry_ref):
    @pl.when(pl.program_id(2) == 0)
    def _():
        carry_ref[...] = st0_ref[0]

    prod = ac_ref[0] * ah_ref[0]
    tl = prod.shape[0]
    row = lax.broadcasted_iota(jnp.int32, prod.shape, 0)
    c6 = carry_ref[6:7, :]
    c7 = carry_ref[7:8, :]
    p1 = jnp.where(row == 0, c7, pltpu.roll(prod, 1, axis=0))
    p2 = jnp.where(row == 0, c6, jnp.where(row == 1, c7, pltpu.roll(prod, 2, axis=0)))
    w = w_ref[...]
    u = w[2:3, :] * prod + w[1:2, :] * p1 + w[0:1, :] * p2
    y_ref[0] = (ab_ref[0] * u * _silu(az_ref[0])).astype(y_ref.dtype)
    last = prod[tl - 8:, :]
    carry_ref[...] = last
    st_ref[0] = last


def _mixer_a(proj3, conv_w8, state8, tl, tc):
    b, l, _ = proj3.shape
    nct = D_MODEL // tc

    def col(gi):
        return pl.BlockSpec((1, tl, tc), lambda bi, ci, li, gi=gi: (bi, li, gi * nct + ci))

    return pl.pallas_call(
        _mixer_a_body,
        grid=(b, nct, l // tl),
        in_specs=[col(0), col(1), col(2), col(3),
                  pl.BlockSpec((8, tc), lambda bi, ci, li: (0, ci)),
                  pl.BlockSpec((1, 8, tc), lambda bi, ci, li: (bi, 0, ci))],
        out_specs=[pl.BlockSpec((1, tl, tc), lambda bi, ci, li: (bi, li, ci)),
                   pl.BlockSpec((1, 8, tc), lambda bi, ci, li: (bi, 0, ci))],
        out_shape=[jax.ShapeDtypeStruct((b, l, D_MODEL), BF16),
                   jax.ShapeDtypeStruct((b, 8, D_MODEL), F32)],
        scratch_shapes=[pltpu.VMEM((8, tc), F32)],
        compiler_params=_cparams("parallel", "parallel", "arbitrary"),
        name="mixer_a",
    )(proj3, proj3, proj3, proj3, conv_w8, state8)


def _cumsum_rows(x):
    n = x.shape[0]
    row = lax.broadcasted_iota(jnp.int32, x.shape, 0)
    s = 1
    while s < n:
        x = x + jnp.where(row >= s, pltpu.roll(x, s, axis=0), 0.0)
        s *= 2
    return x


def _delta_chunk(q, k, v, gcol, bcol, s_prev):
    c = q.shape[0]
    ri = lax.broadcasted_iota(jnp.int32, (c, c), 0)
    ci = lax.broadcasted_iota(jnp.int32, (c, c), 1)
    grow = gcol.T[:c, :]
    decay = jnp.where(ri >= ci, jnp.exp(jnp.minimum(gcol[:, :c] - grow, 0.0)), 0.0)
    kb = k * bcol
    k16 = k.astype(BF16)
    a_kk = jnp.where(ri > ci, _dot_nt(kb.astype(BF16), k16) * decay, 0.0)
    a_qk = _dot_nt(q.astype(BF16), k16) * decay
    t_inv = jnp.where(ri == ci, 1.0, 0.0) - a_kk
    power = a_kk
    span = 2
    while span < c:
        p16 = power.astype(BF16)
        power = _dot(p16, p16)
        t_inv = t_inv + _dot(t_inv.astype(BF16), power.astype(BF16))
        span *= 2
    eg = jnp.exp(gcol)
    rhs = jnp.concatenate([v * bcol, kb * eg], axis=1)
    sol = _dot(t_inv.astype(BF16), rhs.astype(BF16))
    u, w = sol[:, :HEAD_DIM], sol[:, HEAD_DIM:]
    s16 = s_prev.astype(BF16)
    v_new = u - _dot(w.astype(BF16), s16)
    vn16 = v_new.astype(BF16)
    o = _dot((q * eg).astype(BF16), s16) + _dot(a_qk.astype(BF16), vn16)
    g_last = gcol[c - 1:c, :]
    k_dec = k * jnp.exp(g_last - gcol)
    s_new = s_prev * jnp.exp(g_last) + _dot_tn(k_dec.astype(BF16), vn16)
    return o, s_new


def _delta_body(q_ref, k_ref, v_ref, z_ref, gb_ref, wq_ref, wk_ref, wv_ref, cq_ref, ck_ref, cv_ref, s0_ref,
                ng_ref, y_ref, s_ref, s_scr, carry, *, rows, chunk):
    lb = q_ref.shape[1]
    n_chunks = lb // rows

    @pl.when(pl.program_id(2) == 0)
    def _():
        s_scr[...] = s0_ref[0]
        carry[0] = cq_ref[0]
        carry[1] = ck_ref[0]
        carry[2] = cv_ref[0]

    def conv_silu(x_ref, w_ref, which, lanes, r0, first):
        cur = x_ref[0, pl.ds(r0, rows), lanes]
        if n_chunks == 1:
            prev = carry[which, :, lanes]
        else:
            p0 = pl.multiple_of(jnp.maximum(r0 - 8, 0), 8)
            prev = jnp.where(first, carry[which, :, lanes], x_ref[0, pl.ds(p0, 8), lanes])
        ext = jnp.concatenate([prev, cur], axis=0)
        w = w_ref[:, lanes]
        y = (w[3:4, :] * cur + w[2:3, :] * ext[7:7 + rows, :] + w[1:2, :] * ext[6:6 + rows, :]
             + w[0:1, :] * ext[5:5 + rows, :])
        return _silu(y)

    def pad(x):
        if rows == chunk:
            return x
        return jnp.concatenate([x, jnp.zeros((chunk - rows, x.shape[1]), x.dtype)], axis=0)

    def do_chunk(ci):
        if n_chunks == 1:
            r0, first = 0, True
        else:
            r0, first = pl.multiple_of(ci * rows, rows), ci == 0
        gates = pad(gb_ref[0, 0, pl.ds(r0, rows), :])
        gsum = _cumsum_rows(gates)
        for j in range(HEAD_GROUP):
            lanes = slice(j * HEAD_DIM, (j + 1) * HEAD_DIM)
            q = conv_silu(q_ref, wq_ref, 0, lanes, r0, first)
            k = conv_silu(k_ref, wk_ref, 1, lanes, r0, first)
            v = conv_silu(v_ref, wv_ref, 2, lanes, r0, first)
            q = q * lax.rsqrt(jnp.sum(q * q, axis=-1, keepdims=True) + NORM_EPS) * (HEAD_DIM ** -0.5)
            k = k * lax.rsqrt(jnp.sum(k * k, axis=-1, keepdims=True) + NORM_EPS)
            bcol = jnp.broadcast_to(gates[:, j:j + 1], (chunk, HEAD_DIM))
            gcol = jnp.broadcast_to(gsum[:, HEAD_GROUP + j:HEAD_GROUP + j + 1], (chunk, HEAD_DIM))
            o, s_new = _delta_chunk(pad(q), pad(k), pad(v), gcol, bcol, s_scr[j])
            s_scr[j] = s_new
            o = _rms_rows(o[:rows, :], ng_ref[...]) * _silu(z_ref[0, pl.ds(r0, rows), lanes])
            y_ref[0, pl.ds(r0, rows), lanes] = o.astype(y_ref.dtype)

    if n_chunks == 1:
        do_chunk(0)
    else:
        def loop_body(ci, carry_val):
            do_chunk(ci)
            return carry_val
        lax.fori_loop(0, n_chunks, loop_body, 0)
        carry[0] = q_ref[0, lb - 8:, :]
        carry[1] = k_ref[0, lb - 8:, :]
        carry[2] = v_ref[0, lb - 8:, :]

    s_ref[0] = s_scr[...]


def _delta_net(proj3, gates4, conv_w8, conv_state8, s0, norm_g, lb):
    b, l, _ = proj3.shape
    gw = HEAD_GROUP * HEAD_DIM
    ngrp = HEADS // HEAD_GROUP
    per_d = D_MODEL // gw
    rows = min(lb, DN_CHUNK)

    def col(gi):
        return pl.BlockSpec((1, lb, gw), lambda bi, hi, li, gi=gi: (bi, li, gi * per_d + hi))

    def wcol(gi):
        return pl.BlockSpec((8, gw), lambda bi, hi, li, gi=gi: (0, gi * per_d + hi))

    def scol(gi):
        return pl.BlockSpec((1, 8, gw), lambda bi, hi, li, gi=gi: (bi, 0, gi * per_d + hi))

    state_spec = pl.BlockSpec((1, HEAD_GROUP, HEAD_DIM, HEAD_DIM), lambda bi, hi, li: (bi, hi, 0, 0))
    return pl.pallas_call(
        functools.partial(_delta_body, rows=rows, chunk=DN_CHUNK),
        grid=(b, ngrp, l // lb),
        in_specs=[col(4), col(5), col(6), col(7),
                  pl.BlockSpec((1, 1, lb, 128), lambda bi, hi, li: (bi, hi, li, 0)),
                  wcol(0), wcol(1), wcol(2), scol(0), scol(1), scol(2),
                  state_spec,
                  pl.BlockSpec((1, HEAD_DIM), lambda bi, hi, li: (0, 0))],
        out_specs=[pl.BlockSpec((1, lb, gw), lambda bi, hi, li: (bi, li, hi)), state_spec],
        out_shape=[jax.ShapeDtypeStruct((b, l, D_MODEL), BF16),
                   jax.ShapeDtypeStruct((b, HEADS, HEAD_DIM, HEAD_DIM), F32)],
        scratch_shapes=[pltpu.VMEM((HEAD_GROUP, HEAD_DIM, HEAD_DIM), F32),
                        pltpu.VMEM((3, 8, gw), F32)],
        compiler_params=_cparams("parallel", "parallel", "arbitrary"),
        name="delta_net",
    )(proj3, proj3, proj3, proj3, gates4, conv_w8, conv_w8, conv_w8, conv_state8, conv_state8, conv_state8,
      s0, norm_g)


def _mla_qkv_body(cq_ref, ckv_ref, pair_ref, cos_ref, sin_ref, qg_ref, kvg_ref, wm_ref, ws_ref, wuk_ref, wuv_ref,
                  q_ref, ckv_out, kpe_out, kf_ref, v_ref, *, expand_kv):
    cos_t = cos_ref[...]
    sin_t = sin_ref[...]
    cqn = _rms_rows(cq_ref[...], qg_ref[...]).astype(BF16)
    ckv = _rms_rows(ckv_ref[...], kvg_ref[...])
    ckv_out[...] = ckv
    pair = pair_ref[...]
    kpe = pair[:, :128] * cos_t + pair[:, 128:] * sin_t
    kpe_out[...] = kpe
    ckv16 = ckv.astype(BF16)
    kpe16 = kpe.astype(BF16)
    for h in range(HEADS):
        qm = _dot(cqn, wm_ref[:, h * QK_PAD:(h + 1) * QK_PAD])
        qs = _dot(cqn, ws_ref[:, h * 128:(h + 1) * 128])
        q_ref[:, h * QK_PAD:h * QK_PAD + 128] = (qm[:, :128] * MLA_SCALE).astype(BF16)
        q_ref[:, h * QK_PAD + 128:(h + 1) * QK_PAD] = ((qm[:, 128:] * cos_t + qs * sin_t) * MLA_SCALE).astype(BF16)
        if expand_kv:
            kf_ref[:, h * QK_PAD:h * QK_PAD + 128] = _dot(ckv16, wuk_ref[:, h * 128:(h + 1) * 128]).astype(BF16)
            kf_ref[:, h * QK_PAD + 128:(h + 1) * QK_PAD] = kpe16
            v_ref[:, h * 128:(h + 1) * 128] = _dot(ckv16, wuv_ref[:, h * 128:(h + 1) * 128]).astype(BF16)
    if not expand_kv:
        kf_ref[...] = jnp.zeros(kf_ref.shape, kf_ref.dtype)
        v_ref[...] = jnp.zeros(v_ref.shape, v_ref.dtype)


def _mla_qkv(proj, cos_t, sin_t, qg, kvg, wm, ws, wuk, wuv, tm, expand_kv):
    t = proj.shape[0]
    kf_shape = (t, HEADS * QK_PAD) if expand_kv else (8, 128)
    v_shape = (t, HEADS * V_DIM) if expand_kv else (8, 128)
    kf_spec = (pl.BlockSpec((tm, HEADS * QK_PAD), lambda i: (i, 0)) if expand_kv
               else pl.BlockSpec((8, 128), lambda i: (0, 0)))
    v_spec = (pl.BlockSpec((tm, HEADS * V_DIM), lambda i: (i, 0)) if expand_kv
              else pl.BlockSpec((8, 128), lambda i: (0, 0)))
    const = lambda shape: pl.BlockSpec(shape, lambda i: (0, 0))
    return pl.pallas_call(
        functools.partial(_mla_qkv_body, expand_kv=expand_kv),
        grid=(t // tm,),
        in_specs=[pl.BlockSpec((tm, Q_LORA), lambda i: (i, 0)),
                  pl.BlockSpec((tm, KV_LORA), lambda i: (i, 1)),
                  pl.BlockSpec((tm, 256), lambda i: (i, PAIR_COL0 // 256)),
                  pl.BlockSpec((tm, 128), lambda i: (i, 0)),
                  pl.BlockSpec((tm, 128), lambda i: (i, 0)),
                  const((1, Q_LORA)), const((1, KV_LORA)),
                  const(wm.shape), const(ws.shape), const(wuk.shape), const(wuv.shape)],
        out_specs=[pl.BlockSpec((tm, HEADS * QK_PAD), lambda i: (i, 0)),
                   pl.BlockSpec((tm, KV_LORA), lambda i: (i, 0)),
                   pl.BlockSpec((tm, 128), lambda i: (i, 0)),
                   kf_spec, v_spec],
        out_shape=[jax.ShapeDtypeStruct((t, HEADS * QK_PAD), BF16),
                   jax.ShapeDtypeStruct((t, KV_LORA), F32),
                   jax.ShapeDtypeStruct((t, 128), F32),
                   jax.ShapeDtypeStruct(kf_shape, BF16),
                   jax.ShapeDtypeStruct(v_shape, BF16)],
        compiler_params=_cparams("arbitrary"),
        name="mla_qkv",
    )(proj, proj, proj, cos_t, sin_t, qg, kvg, wm, ws, wuk, wuv)


def _flash_body(q_ref, k_ref, v_ref, gate_ref, o_ref, m_ref, l_ref, acc_ref, *, tq):
    qi = pl.program_id(2)
    q = q_ref[0]
    m_ref[...] = jnp.full(m_ref.shape, -jnp.inf, F32)
    l_ref[...] = jnp.zeros(l_ref.shape, F32)
    acc_ref[...] = jnp.zeros(acc_ref.shape, F32)

    def step(j, masked):
        r0 = pl.multiple_of(j * tq, tq)
        s = _dot_nt(q, k_ref[0, pl.ds(r0, tq), :])
        if masked:
            ri = lax.broadcasted_iota(jnp.int32, s.shape, 0)
            ci = lax.broadcasted_iota(jnp.int32, s.shape, 1)
            s = jnp.where(ci <= ri, s, -jnp.inf)
        m_prev = m_ref[...]
        m_new = jnp.maximum(m_prev, jnp.max(s, axis=1, keepdims=True))
        alpha = jnp.exp(m_prev - m_new)
        e = jnp.exp(s - m_new[:, :1])
        l_ref[...] = alpha * l_ref[...] + jnp.sum(e, axis=1, keepdims=True)
        acc_ref[...] = alpha * acc_ref[...] + _dot(e.astype(BF16), v_ref[0, pl.ds(r0, tq), :])
        m_ref[...] = m_new

    def loop_body(j, carry_val):
        step(j, False)
        return carry_val

    lax.fori_loop(0, qi, loop_body, 0)
    step(qi, True)
    o_ref[0] = (acc_ref[...] / l_ref[...] * _silu(gate_ref[0])).astype(o_ref.dtype)


def _prompt_attention(q3, k3, v3, proj3, tq):
    b, l, _ = q3.shape
    return pl.pallas_call(
        functools.partial(_flash_body, tq=tq),
        grid=(b, HEADS, l // tq),
        in_specs=[pl.BlockSpec((1, tq, QK_PAD), lambda bi, h, qi: (bi, qi, h)),
                  pl.BlockSpec((1, l, QK_PAD), lambda bi, h, qi: (bi, 0, h)),
                  pl.BlockSpec((1, l, V_DIM), lambda bi, h, qi: (bi, 0, h)),
                  pl.BlockSpec((1, tq, V_DIM), lambda bi, h, qi: (bi, qi, GATE_COL0 // V_DIM + h))],
        out_specs=pl.BlockSpec((1, tq, V_DIM), lambda bi, h, qi: (bi, qi, h)),
        out_shape=jax.ShapeDtypeStruct((b, l, HEADS * V_DIM), BF16),
        scratch_shapes=[pltpu.VMEM((tq, 128), F32), pltpu.VMEM((tq, 128), F32), pltpu.VMEM((tq, V_DIM), F32)],
        compiler_params=_cparams("parallel", "parallel", "arbitrary"),
        name="prompt_attention",
    )(q3, k3, v3, proj3)


def _head_mm_body(x_ref, w_ref, o_ref):
    o_ref[0] = _dot(x_ref[...], w_ref[0]).astype(o_ref.dtype)


def _absorb_queries(q, w_ukt):
    t = q.shape[0]
    return pl.pallas_call(
        _head_mm_body,
        grid=(HEADS,),
        in_specs=[pl.BlockSpec((t, NOPE), lambda h: (0, 2 * h)),
                  pl.BlockSpec((1, NOPE, KV_LORA), lambda h: (h, 0, 0))],
        out_specs=pl.BlockSpec((1, t, KV_LORA), lambda h: (h, 0, 0)),
        out_shape=jax.ShapeDtypeStruct((HEADS, t, KV_LORA), BF16),
        compiler_params=_cparams("parallel"),
        name="absorb_queries",
    )(q, w_ukt)


def _expand_values_body(x_ref, w_ref, gate_ref, o_ref):
    o_ref[...] = (_dot(x_ref[0], w_ref[0]) * _silu(gate_ref[...])).astype(o_ref.dtype)


def _expand_values(o_lat, w_uvh, proj):
    t = o_lat.shape[1]
    return pl.pallas_call(
        _expand_values_body,
        grid=(HEADS,),
        in_specs=[pl.BlockSpec((1, t, KV_LORA), lambda h: (h, 0, 0)),
                  pl.BlockSpec((1, KV_LORA, V_DIM), lambda h: (h, 0, 0)),
                  pl.BlockSpec((t, V_DIM), lambda h: (0, GATE_COL0 // V_DIM + h))],
        out_specs=pl.BlockSpec((t, V_DIM), lambda h: (0, h)),
        out_shape=jax.ShapeDtypeStruct((t, HEADS * V_DIM), BF16),
        compiler_params=_cparams("parallel"),
        name="expand_values",
    )(o_lat, w_uvh, proj)


def _paged_attn_body(pt_ref, ql_ref, qr_ref, cn_ref, rn_ref, *rest, dec_seq):
    npg = PAGES_PER_STEP
    c_refs, r_refs = rest[:npg], rest[npg:2 * npg]
    o_ref, m_ref, l_ref, acc_ref = rest[2 * npg:]
    j = pl.program_id(1)
    ql = ql_ref[0]
    qr = qr_ref[0]

    def update(scores, values):
        s = jnp.concatenate(scores, axis=1) if len(scores) > 1 else scores[0]
        m_prev = m_ref[...]
        m_new = jnp.maximum(m_prev, jnp.max(s, axis=1, keepdims=True))
        alpha = jnp.exp(m_prev - m_new)
        l_new = alpha * l_ref[...]
        acc = alpha[:, :1] * acc_ref[...]
        for i, (sc, val) in enumerate(zip(scores, values)):
            e = jnp.exp(sc - m_new[:, :1])
            l_new = l_new + jnp.sum(e, axis=1, keepdims=True)
            acc = acc + _dot(e.astype(BF16), val)
        l_ref[...] = l_new
        acc_ref[...] = acc
        m_ref[...] = m_new

    @pl.when(j == 0)
    def _():
        m_ref[...] = jnp.full(m_ref.shape, -jnp.inf, F32)
        l_ref[...] = jnp.zeros(l_ref.shape, F32)
        acc_ref[...] = jnp.zeros(acc_ref.shape, F32)
        c = cn_ref[0].astype(BF16)
        s = _dot_nt(ql, c) + _dot_nt(qr, rn_ref[0].astype(BF16))
        ri = lax.broadcasted_iota(jnp.int32, s.shape, 0)
        ci = lax.broadcasted_iota(jnp.int32, s.shape, 1)
        s = jnp.where(ci <= ri % dec_seq, s, -jnp.inf)
        update([s], [c])

    scores, values = [], []
    for c_ref, r_ref in zip(c_refs, r_refs):
        c = c_ref[0, 0].astype(BF16)
        scores.append(_dot_nt(ql, c) + _dot_nt(qr, r_ref[0, 0].astype(BF16)))
        values.append(c)
    update(scores, values)

    @pl.when(j == pl.num_programs(1) - 1)
    def _():
        o_ref[0] = (acc_ref[...] / l_ref[...][:, :1]).astype(o_ref.dtype)


def _paged_attention(q_lat, q_r, ckv_new, kpe_new, cache_ckv, cache_kpe, page_table_flat, n_pages, dec_seq):
    b, rows, _ = q_lat.shape
    npg = PAGES_PER_STEP

    def page_spec(width, i):
        return pl.BlockSpec((1, 1, PAGE_SIZE, width),
                            lambda bi, j, pt, i=i: (0, pt[bi * n_pages + j * npg + i], 0, 0))

    grid_spec = pltpu.PrefetchScalarGridSpec(
        num_scalar_prefetch=1,
        grid=(b, n_pages // npg),
        in_specs=([pl.BlockSpec((1, rows, KV_LORA), lambda bi, j, pt: (bi, 0, 0)),
                   pl.BlockSpec((1, rows, ROPE), lambda bi, j, pt: (bi, 0, 0)),
                   pl.BlockSpec((1, PAGE_SIZE, KV_LORA), lambda bi, j, pt: (bi, 0, 0)),
                   pl.BlockSpec((1, PAGE_SIZE, ROPE), lambda bi, j, pt: (bi, 0, 0))]
                  + [page_spec(KV_LORA, i) for i in range(npg)]
                  + [page_spec(ROPE, i) for i in range(npg)]),
        out_specs=pl.BlockSpec((1, rows, KV_LORA), lambda bi, j, pt: (bi, 0, 0)),
        scratch_shapes=[pltpu.VMEM((rows, 128), F32), pltpu.VMEM((rows, 128), F32),
                        pltpu.VMEM((rows, KV_LORA), F32)],
    )
    return pl.pallas_call(
        functools.partial(_paged_attn_body, dec_seq=dec_seq),
        grid_spec=grid_spec,
        out_shape=jax.ShapeDtypeStruct((b, rows, KV_LORA), BF16),
        compiler_params=_cparams("parallel", "arbitrary"),
        name="paged_attention",
    )(page_table_flat, q_lat, q_r, ckv_new, kpe_new, *([cache_ckv] * npg), *([cache_kpe] * npg))


def _prepare_weights(norm_g, ple_norm_g, w_ple_gate, w_ple_proj, final_norm_g, w_in_even, w_out_even, conv_a_w,
                     conv_qkv_w, dn_a_log, dn_dt_bias, dn_norm_g, w_in_mla, mla_q_norm_g, mla_kv_norm_g, w_uq,
                     w_uk, w_uv, w_o_mla):
    d = D_MODEL
    w = {}
    w["norm_g"] = norm_g.reshape(2, 1, d)
    w["ple_norm_g"] = ple_norm_g.reshape(2, 1, d)
    w["final_norm_g"] = final_norm_g.reshape(1, d)
    w["w_ple_gate"] = w_ple_gate.astype(BF16)
    w["w_ple_proj"] = w_ple_proj.astype(BF16)
    w_in = w_in_even[0]
    w["w_in_main"] = w_in[:, :8 * d].astype(BF16)
    w["w_in_gates"] = jnp.pad(w_in[:, 8 * d:], ((0, 0), (0, 128 - 2 * HEADS))).astype(BF16)
    w["a_log_row"] = jnp.pad(dn_a_log[0], (HEADS, 128 - 2 * HEADS)).reshape(1, 128)
    w["dt_bias_row"] = jnp.pad(dn_dt_bias[0], (HEADS, 128 - 2 * HEADS)).reshape(1, 128)
    w["w_out_a"] = w_out_even[0, :d].astype(BF16)
    w["w_out_b"] = w_out_even[0, d:].astype(BF16)
    w["conv_a_w8"] = jnp.pad(conv_a_w[0], ((0, 8 - SC_CONV), (0, 0)))
    w["conv_qkv_w8"] = jnp.pad(conv_qkv_w[0], ((0, 8 - DN_CONV), (0, 0)))
    w["dn_norm_g"] = dn_norm_g[0].reshape(1, HEAD_DIM)
    wi = w_in_mla[0]
    k_pe = wi[:, 2 * Q_LORA:2 * Q_LORA + ROPE]
    half = ROPE // 2
    zeros64 = jnp.zeros((d, 128 - ROPE), F32)
    w["w_in_mla"] = jnp.concatenate(
        [wi[:, :2 * Q_LORA], wi[:, 2 * Q_LORA + ROPE:], k_pe, zeros64, k_pe[:, half:], k_pe[:, :half], zeros64,
         jnp.zeros((d, MLA_IN_PAD - PAIR_COL0 - 256), F32)], axis=1).astype(BF16)
    w["mla_q_norm_g"] = mla_q_norm_g[0].reshape(1, Q_LORA)
    w["mla_kv_norm_g"] = mla_kv_norm_g[0].reshape(1, KV_LORA)
    uq = w_uq[0].reshape(Q_LORA, HEADS, NOPE + ROPE)
    x1, x2 = uq[:, :, NOPE:NOPE + half], uq[:, :, NOPE + half:]
    zpad = jnp.zeros((Q_LORA, HEADS, QK_PAD - NOPE - ROPE), F32)
    w["w_uq_main"] = jnp.concatenate([uq, zpad], axis=2).reshape(Q_LORA, HEADS * QK_PAD).astype(BF16)
    w["w_uq_swap"] = jnp.concatenate([x2, x1, zpad], axis=2).reshape(Q_LORA, HEADS * 128).astype(BF16)
    w["w_uk"] = w_uk[0].reshape(KV_LORA, HEADS * NOPE).astype(BF16)
    w["w_uv"] = w_uv[0].reshape(KV_LORA, HEADS * V_DIM).astype(BF16)
    w["w_uk_t"] = jnp.transpose(w_uk[0], (1, 2, 0)).astype(BF16)
    w["w_uv_h"] = jnp.transpose(w_uv[0], (1, 0, 2)).astype(BF16)
    w["w_o_mla"] = w_o_mla[0].astype(BF16)
    return w


def _rope_tables(pos):
    half = ROPE // 2
    inv = ROPE_THETA ** (-jnp.arange(half, dtype=F32) / half)
    ang = pos.astype(F32)[:, None] * inv
    cos, sin = jnp.cos(ang), jnp.sin(ang)
    z = jnp.zeros((pos.shape[0], 128 - ROPE), F32)
    return jnp.concatenate([cos, cos, z], axis=1), jnp.concatenate([-sin, sin, z], axis=1)


def _layer_stack(x, p, pos, conv_a0, conv_qkv0, delta0, past, w, tm):
    b, l, d = x.shape
    t = b * l
    h = x.reshape(t, d)
    lp = max(l, 8)

    proj = _rms_matmul(h, w["norm_g"][0], w["w_in_main"], F32, tm, 1024, "in_proj_even")
    gates = _rms_matmul_gates(h, w["norm_g"][0], w["w_in_gates"], w["a_log_row"], w["dt_bias_row"], tm)
    proj3 = proj.reshape(b, l, 8 * d)
    gates3 = gates.reshape(b, l, 128)
    if lp != l:
        proj3 = jnp.pad(proj3, ((0, 0), (0, lp - l), (0, 0)))
        gates3 = jnp.pad(gates3, ((0, 0), (0, lp - l), (0, 0)))
    ngrp = HEADS // HEAD_GROUP
    beta4 = gates3[:, :, :HEADS].reshape(b, lp, ngrp, HEAD_GROUP)
    g4 = gates3[:, :, HEADS:2 * HEADS].reshape(b, lp, ngrp, HEAD_GROUP)
    gates4 = jnp.transpose(jnp.concatenate([beta4, g4], axis=-1), (0, 2, 1, 3))
    gates4 = jnp.pad(gates4, ((0, 0), (0, 0), (0, 0), (0, 128 - 2 * HEAD_GROUP)))

    state_a8 = jnp.pad(conv_a0, ((0, 0), (8 - (SC_CONV - 1), 0), (0, 0)))
    state_qkv8 = jnp.pad(conv_qkv0, ((0, 0), (8 - (DN_CONV - 1), 0), (0, 0)))
    tl = min(lp, 512)
    y_a, a_last8 = _mixer_a(proj3, w["conv_a_w8"], state_a8, tl, 512)
    y_b, delta1 = _delta_net(proj3, gates4, w["conv_qkv_w8"], state_qkv8, delta0, w["dn_norm_g"], tl)
    if lp != l:
        y_a, y_b = y_a[:, :l], y_b[:, :l]
        conv_a1 = a_last8[:, l - (SC_CONV - 1):l]
    else:
        conv_a1 = a_last8[:, 8 - (SC_CONV - 1):]
    conv_qkv1 = proj3[:, l - (DN_CONV - 1):l, 4 * d:7 * d]
    h = _matmul_residual([y_a.reshape(t, d), y_b.reshape(t, d)], [w["w_out_a"], w["w_out_b"]], h, min(tm, 512),
                         1024, "out_proj_even")
    h = _ple(h, w["ple_norm_g"][0], w["w_ple_gate"][0], p[0].reshape(t, PLE_DIM), w["w_ple_proj"][0],
             w["final_norm_g"], False, min(tm, 512), "ple0")

    proj = _rms_matmul(h, w["norm_g"][1], w["w_in_mla"], F32, tm, 512, "in_proj_mla")
    cos_t, sin_t = _rope_tables(pos)
    cos_t = jnp.broadcast_to(cos_t[None], (b, l, 128)).reshape(t, 128)
    sin_t = jnp.broadcast_to(sin_t[None], (b, l, 128)).reshape(t, 128)
    q, ckv, kpe128, kfull, v = _mla_qkv(proj, cos_t, sin_t, w["mla_q_norm_g"], w["mla_kv_norm_g"], w["w_uq_main"],
                                        w["w_uq_swap"], w["w_uk"], w["w_uv"], min(t, 256), past is None)
    kpe = kpe128[:, :ROPE]
    if past is None:
        o = _prompt_attention(q.reshape(b, l, HEADS * QK_PAD), kfull.reshape(b, l, HEADS * QK_PAD),
                              v.reshape(b, l, HEADS * V_DIM), proj.reshape(b, l, MLA_IN_PAD), 512)
        o = o.reshape(t, HEADS * V_DIM)
    else:
        cache_ckv, cache_kpe, page_table = past
        n_pages = page_table.shape[1]
        q_lat = _absorb_queries(q, w["w_uk_t"])
        q_lat = jnp.transpose(q_lat.reshape(HEADS, b, l, KV_LORA), (1, 0, 2, 3)).reshape(b, HEADS * l, KV_LORA)
        q_r = q.reshape(b, l, HEADS, QK_PAD)[:, :, :, NOPE:NOPE + ROPE]
        q_r = jnp.transpose(q_r, (0, 2, 1, 3)).reshape(b, HEADS * l, ROPE)
        ckv_new = jnp.pad(ckv.reshape(b, l, KV_LORA), ((0, 0), (0, PAGE_SIZE - l), (0, 0)))
        kpe_new = jnp.pad(kpe.reshape(b, l, ROPE), ((0, 0), (0, PAGE_SIZE - l), (0, 0)))
        o_lat = _paged_attention(q_lat, q_r, ckv_new, kpe_new, cache_ckv, cache_kpe, page_table.reshape(-1),
                                 n_pages, l)
        o_lat = jnp.transpose(o_lat.reshape(b, HEADS, l, KV_LORA), (1, 0, 2, 3)).reshape(HEADS, t, KV_LORA)
        o = _expand_values(o_lat, w["w_uv_h"], proj)
    h = _matmul_residual([o], [w["w_o_mla"]], h, min(tm, 512), 1024, "out_proj_mla")
    y = _ple(h, w["ple_norm_g"][1], w["w_ple_gate"][1], p[1].reshape(t, PLE_DIM), w["w_ple_proj"][1],
             w["final_norm_g"], True, min(tm, 512), "ple1")
    return (y.reshape(b, l, d), conv_a1[None], conv_qkv1[None], delta1[None],
            ckv.reshape(1, b, l, KV_LORA), kpe.reshape(1, b, l, ROPE))


def kernel(x_prompt, x_sample, cache_ckv, cache_kpe, state_conv_a, state_conv_qkv, state_delta, page_table,
           p_prompt, p_sample, norm_g, ple_norm_g, w_ple_gate, w_ple_proj, final_norm_g, w_in_even, w_out_even,
           conv_a_w, conv_qkv_w, dn_a_log, dn_dt_bias, dn_norm_g, w_in_mla, mla_q_norm_g, mla_kv_norm_g, w_uq,
           w_uk, w_uv, w_o_mla):
    w = _prepare_weights(norm_g, ple_norm_g, w_ple_gate, w_ple_proj, final_norm_g, w_in_even, w_out_even, conv_a_w,
                         conv_qkv_w, dn_a_log, dn_dt_bias, dn_norm_g, w_in_mla, mla_q_norm_g, mla_kv_norm_g, w_uq,
                         w_uk, w_uv, w_o_mla)
    bsz, seq, d = x_prompt.shape
    dec_b, dec_l, _ = x_sample.shape
    past_len = page_table.shape[1] * PAGE_SIZE
    conv_ch = state_conv_qkv.shape[-1]
    zero_a = jnp.zeros((bsz, SC_CONV - 1, d), F32)
    zero_qkv = jnp.zeros((bsz, DN_CONV - 1, conv_ch), F32)
    zero_delta = jnp.zeros((bsz, HEADS, HEAD_DIM, HEAD_DIM), F32)
    outs_p = _layer_stack(x_prompt, p_prompt, jnp.arange(seq), zero_a, zero_qkv, zero_delta, None, w, 1024)
    outs_s = _layer_stack(x_sample, p_sample, past_len + jnp.arange(dec_l), state_conv_a[0], state_conv_qkv[0],
                          state_delta[0], (cache_ckv, cache_kpe, page_table), w, dec_b * dec_l)
    return (outs_p[0], outs_s[0]) + tuple(outs_p[1:]) + tuple(outs_s[1:])
```

```python
import functools
import math

import jax
import jax.numpy as jnp
from jax import lax
from jax.experimental import pallas as pl
from jax.experimental.pallas import tpu as pltpu

F32 = jnp.float32
BF16 = jnp.bfloat16

NORM_EPS = 1e-6
D_MODEL = 2048
HEADS = 16
HEAD_DIM = 128
DN_CHUNK = 64
DN_CONV = 4
SC_CONV = 3
Q_LORA = 512
KV_LORA = 512
NOPE = 128
ROPE = 64
V_DIM = 128
PLE_DIM = 256
ROPE_THETA = 10000.0
PAGE_SIZE = 128
MLA_SCALE = (NOPE + ROPE) ** -0.5
QK_PAD = 256
MLA_IN_PAD = 3584
GATE_COL0 = Q_LORA + KV_LORA
PAIR_COL0 = GATE_COL0 + HEADS * V_DIM
HEAD_GROUP = 8
SEG_ROWS = 8
PAGES_PER_STEP = 16

V7X_VMEM_LIMIT_BYTES = 56 * 1024 * 1024


def _cparams(*sem):
    return pltpu.CompilerParams(dimension_semantics=sem, vmem_limit_bytes=V7X_VMEM_LIMIT_BYTES)


def _sigmoid(x):
    return 1.0 / (1.0 + jnp.exp(-x))


def _silu(x):
    return x * _sigmoid(x)


def _rms_rows(x, g):
    return x * lax.rsqrt(jnp.mean(x * x, axis=-1, keepdims=True) + NORM_EPS) * g


def _dot(a, b):
    return jnp.dot(a, b, preferred_element_type=F32)


def _dot_nt(a, b):
    return lax.dot_general(a, b, (((1,), (1,)), ((), ())), preferred_element_type=F32)


def _rms_mm_body(x_ref, g_ref, w_ref, o_ref, xn_ref):
    @pl.when(pl.program_id(1) == 0)
    def _():
        xn_ref[...] = _rms_rows(x_ref[...], g_ref[...]).astype(BF16)

    o_ref[...] = _dot(xn_ref[...], w_ref[...]).astype(o_ref.dtype)


def _rms_matmul(x, g, w, out_dtype, tm, tn, name):
    t, k = x.shape
    n = w.shape[1]
    return pl.pallas_call(
        _rms_mm_body,
        grid=(t // tm, n // tn),
        in_specs=[pl.BlockSpec((tm, k), lambda i, j: (i, 0)),
                  pl.BlockSpec((1, k), lambda i, j: (0, 0)),
                  pl.BlockSpec((k, tn), lambda i, j: (0, j))],
        out_specs=pl.BlockSpec((tm, tn), lambda i, j: (i, j)),
        out_shape=jax.ShapeDtypeStruct((t, n), out_dtype),
        scratch_shapes=[pltpu.VMEM((tm, k), BF16)],
        compiler_params=_cparams("parallel", "arbitrary"),
        name=name,
    )(x, g, w)


def _rms_mm_gates_body(x_ref, g_ref, w_ref, alog_ref, dtb_ref, o_ref):
    xn = _rms_rows(x_ref[...], g_ref[...]).astype(BF16)
    y = _dot(xn, w_ref[...])
    lane = lax.broadcasted_iota(jnp.int32, y.shape, 1)
    beta = _sigmoid(y)
    t = y + dtb_ref[...]
    softplus = jnp.maximum(t, 0.0) + jnp.log1p(jnp.exp(-jnp.abs(t)))
    decay = -jnp.exp(alog_ref[...]) * softplus
    o_ref[...] = jnp.where(lane < HEADS, beta, decay)


def _rms_matmul_gates(x, g, w, alog_row, dtb_row, tm):
    t, k = x.shape
    return pl.pallas_call(
        _rms_mm_gates_body,
        grid=(t // tm,),
        in_specs=[pl.BlockSpec((tm, k), lambda i: (i, 0)),
                  pl.BlockSpec((1, k), lambda i: (0, 0)),
                  pl.BlockSpec((k, 128), lambda i: (0, 0)),
                  pl.BlockSpec((1, 128), lambda i: (0, 0)),
                  pl.BlockSpec((1, 128), lambda i: (0, 0))],
        out_specs=pl.BlockSpec((tm, 128), lambda i: (i, 0)),
        out_shape=jax.ShapeDtypeStruct((t, 128), F32),
        compiler_params=_cparams("parallel"),
        name="dn_gates",
    )(x, g, w, alog_row, dtb_row)


def _mm_res_body(*refs, n_in):
    xs, ws, res_ref, o_ref = refs[:n_in], refs[n_in:2 * n_in], refs[2 * n_in], refs[2 * n_in + 1]
    acc = res_ref[...]
    for x_ref, w_ref in zip(xs, ws):
        acc = acc + _dot(x_ref[...], w_ref[...])
    o_ref[...] = acc


def _matmul_residual(xs, ws, res, tm, tn, name):
    t, n = res.shape
    n_in = len(xs)
    in_specs = ([pl.BlockSpec((tm, x.shape[1]), lambda i, j: (i, 0)) for x in xs]
                + [pl.BlockSpec((w.shape[0], tn), lambda i, j: (0, j)) for w in ws]
                + [pl.BlockSpec((tm, tn), lambda i, j: (i, j))])
    return pl.pallas_call(
        functools.partial(_mm_res_body, n_in=n_in),
        grid=(t // tm, n // tn),
        in_specs=in_specs,
        out_specs=pl.BlockSpec((tm, tn), lambda i, j: (i, j)),
        out_shape=jax.ShapeDtypeStruct((t, n), F32),
        compiler_params=_cparams("parallel", "arbitrary"),
        name=name,
    )(*xs, *ws, res)


def _ple_body(h_ref, g_ref, wg_ref, p_ref, wp_ref, fg_ref, o_ref, *, final_norm):
    h = h_ref[...]
    xn = _rms_rows(h, g_ref[...]).astype(BF16)
    gate = _sigmoid(_dot(xn, wg_ref[...]))
    h2 = h + gate * _dot(p_ref[...].astype(BF16), wp_ref[...])
    if final_norm:
        h2 = _rms_rows(h2, fg_ref[...])
    o_ref[...] = h2


def _ple(h, g, w_gate, p, w_proj, final_g, final_norm, tm, name):
    t, d = h.shape
    pd = p.shape[1]
    return pl.pallas_call(
        functools.partial(_ple_body, final_norm=final_norm),
        grid=(t // tm,),
        in_specs=[pl.BlockSpec((tm, d), lambda i: (i, 0)),
                  pl.BlockSpec((1, d), lambda i: (0, 0)),
                  pl.BlockSpec((d, d), lambda i: (0, 0)),
                  pl.BlockSpec((tm, pd), lambda i: (i, 0)),
                  pl.BlockSpec((pd, d), lambda i: (0, 0)),
                  pl.BlockSpec((1, d), lambda i: (0, 0))],
        out_specs=pl.BlockSpec((tm, d), lambda i: (i, 0)),
        out_shape=jax.ShapeDtypeStruct((t, d), F32),
        compiler_params=_cparams("parallel"),
        name=name,
    )(h, g, w_gate, p, w_proj, final_g)


def _mixer_a_body(ac_ref, ah_ref, ab_ref, az_ref, w_ref, y_ref, st_ref, carry_ref, *, emit_prod):
    @pl.when(pl.program_id(2) == 0)
    def _():
        carry_ref[...] = jnp.zeros(carry_ref.shape, F32)

    prod = ac_ref[0] * ah_ref[0]
    tl = prod.shape[0]
    row = lax.broadcasted_iota(jnp.int32, prod.shape, 0)
    c6 = carry_ref[6:7, :]
    c7 = carry_ref[7:8, :]
    p1 = jnp.where(row == 0, c7, pltpu.roll(prod, 1, axis=0))
    p2 = jnp.where(row == 0, c6, jnp.where(row == 1, c7, pltpu.roll(prod, 2, axis=0)))
    w = w_ref[...]
    u = w[2:3, :] * prod + w[1:2, :] * p1 + w[0:1, :] * p2
    y_ref[0] = (ab_ref[0] * u * _silu(az_ref[0])).astype(y_ref.dtype)
    last = prod[tl - 8:, :]
    carry_ref[...] = last
    st_ref[0] = prod if emit_prod else last


def _mixer_a(proj3, conv_w8, tl, tc, emit_prod):
    b, l, _ = proj3.shape
    nct = D_MODEL // tc
    st_rows = tl if emit_prod else 8

    def col(gi):
        return pl.BlockSpec((1, tl, tc), lambda bi, ci, li, gi=gi: (bi, li, gi * nct + ci))

    return pl.pallas_call(
        functools.partial(_mixer_a_body, emit_prod=emit_prod),
        grid=(b, nct, l // tl),
        in_specs=[col(0), col(1), col(2), col(3),
                  pl.BlockSpec((8, tc), lambda bi, ci, li: (0, ci))],
        out_specs=[pl.BlockSpec((1, tl, tc), lambda bi, ci, li: (bi, li, ci)),
                   pl.BlockSpec((1, st_rows, tc), lambda bi, ci, li: (bi, li if emit_prod else 0, ci))],
        out_shape=[jax.ShapeDtypeStruct((b, l, D_MODEL), BF16),
                   jax.ShapeDtypeStruct((b, l if emit_prod else 8, D_MODEL), F32)],
        scratch_shapes=[pltpu.VMEM((8, tc), F32)],
        compiler_params=_cparams("parallel", "parallel", "arbitrary"),
        name="mixer_a",
    )(proj3, proj3, proj3, proj3, conv_w8)


def _segment_scan(x, seg):
    n = x.shape[0]
    row = lax.broadcasted_iota(jnp.int32, x.shape, 0)
    pos = row % seg
    cs = x
    s = 1
    while s < seg:
        cs = cs + jnp.where(pos >= s, pltpu.roll(cs, s, axis=0), 0.0)
        s *= 2
    if seg == n:
        return cs, jnp.broadcast_to(cs[n - 1:n, :], x.shape)
    tot = cs
    s = 1
    while s < seg:
        tot = jnp.where(pos + s < seg, pltpu.roll(tot, n - s, axis=0), tot)
        s *= 2
    return cs, tot


def _delta_prepare(q_ref, k_ref, v_ref, gb_ref, wq_ref, wk_ref, wv_ref, r0, prev_rows, seg):
    c = DN_CHUNK
    heads = range(HEAD_GROUP)
    gates = gb_ref[0, 0, pl.ds(r0, c), :]
    gsum, gtot = _segment_scan(gates, seg)
    ri = lax.broadcasted_iota(jnp.int32, (c, c), 0)
    ci = lax.broadcasted_iota(jnp.int32, (c, c), 1)
    bcols = [jnp.broadcast_to(gates[:, j:j + 1], (c, HEAD_DIM)) for j in heads]
    gcols = [jnp.broadcast_to(gsum[:, HEAD_GROUP + j:HEAD_GROUP + j + 1], (c, HEAD_DIM)) for j in heads]
    glasts = [jnp.broadcast_to(gtot[:, HEAD_GROUP + j:HEAD_GROUP + j + 1], (c, HEAD_DIM)) for j in heads]

    def conv_silu(which, x_ref, w_ref, j):
        lanes = slice(j * HEAD_DIM, (j + 1) * HEAD_DIM)
        cur = x_ref[0, pl.ds(r0, c), lanes]
        ext = jnp.concatenate([prev_rows(which, x_ref, lanes), cur], axis=0)
        w = w_ref[:, lanes]
        y = (w[3:4, :] * cur + w[2:3, :] * ext[7:7 + c, :] + w[1:2, :] * ext[6:6 + c, :]
             + w[0:1, :] * ext[5:5 + c, :])
        return _silu(y)

    qs = [conv_silu(0, q_ref, wq_ref, j) for j in heads]
    ks = [conv_silu(1, k_ref, wk_ref, j) for j in heads]
    vs = [conv_silu(2, v_ref, wv_ref, j) for j in heads]
    qs = [q * lax.rsqrt(jnp.sum(q * q, axis=-1, keepdims=True) + NORM_EPS) * (HEAD_DIM ** -0.5) for q in qs]
    ks = [k * lax.rsqrt(jnp.sum(k * k, axis=-1, keepdims=True) + NORM_EPS) for k in ks]
    kbs = [k * b for k, b in zip(ks, bcols)]
    m1s = [_dot_nt(jnp.concatenate([kb, q], axis=0).astype(BF16), k.astype(BF16))
           for kb, q, k in zip(kbs, qs, ks)]
    decays = []
    for gcol in gcols:
        grow = gcol.T[:c, :]
        decay = jnp.where(ri >= ci, jnp.exp(jnp.minimum(gcol[:, :c] - grow, 0.0)), 0.0)
        if seg < c:
            decay = jnp.where(ri // seg == ci // seg, decay, 0.0)
        decays.append(decay)
    powers = [jnp.where(ri > ci, m1[:c, :] * d, 0.0) for m1, d in zip(m1s, decays)]
    a_qks = [m1[c:, :] * d for m1, d in zip(m1s, decays)]
    t_invs = [jnp.where(ri == ci, 1.0, 0.0) - a for a in powers]
    span = 2
    while span < seg:
        p16s = [p.astype(BF16) for p in powers]
        powers = [_dot(p16, p16) for p16 in p16s]
        t_invs = [t + _dot(t.astype(BF16), p.astype(BF16)) for t, p in zip(t_invs, powers)]
        span *= 2
    egs = [jnp.exp(g) for g in gcols]
    sols = [_dot(t.astype(BF16), jnp.concatenate([v * b, kb * eg], axis=1).astype(BF16))
            for t, v, b, kb, eg in zip(t_invs, vs, bcols, kbs, egs)]
    k_dec_ts = [(k * jnp.exp(gl - g)).T for k, gl, g in zip(ks, glasts, gcols)]
    return [(sol[:, :HEAD_DIM], sol[:, HEAD_DIM:], q * eg, a_qk, k_dec_t, gl)
            for sol, q, eg, a_qk, k_dec_t, gl in zip(sols, qs, egs, a_qks, k_dec_ts, glasts)]


def _delta_seq_body(q_ref, k_ref, v_ref, z_ref, gb_ref, wq_ref, wk_ref, wv_ref, ng_ref, y_ref, s_ref,
                    s_scr, carry, u_scr, wq_scr, ak_scr, eg_scr):
    c = DN_CHUNK
    lb = q_ref.shape[1]
    n_chunks = lb // c

    @pl.when(pl.program_id(2) == 0)
    def _():
        s_scr[...] = jnp.zeros(s_scr.shape, F32)
        carry[...] = jnp.zeros(carry.shape, F32)

    heads = range(HEAD_GROUP)

    def prepare(ci, carry_val):
        r0 = pl.multiple_of(ci * c, c)
        p0 = pl.multiple_of(jnp.maximum(r0 - 8, 0), 8)

        def prev_rows(which, x_ref, lanes):
            return jnp.where(ci == 0, carry[which, :, lanes], x_ref[0, pl.ds(p0, 8), lanes])

        parts = _delta_prepare(q_ref, k_ref, v_ref, gb_ref, wq_ref, wk_ref, wv_ref, r0, prev_rows, c)
        for j, (u, w, qe, a_qk, k_dec_t, glast) in enumerate(parts):
            u_scr[ci, j] = u
            wq_scr[ci, j] = jnp.concatenate([w, qe], axis=0).astype(BF16)
            ak_scr[ci, j] = jnp.concatenate([a_qk, k_dec_t], axis=0).astype(BF16)
            eg_scr[ci, j] = jnp.exp(glast[:8, :])
        return carry_val

    lax.fori_loop(0, n_chunks, prepare, 0)

    def apply(ci, carry_val):
        r0 = pl.multiple_of(ci * c, c)
        s_prevs = [s_scr[j] for j in heads]
        r1s = [_dot(wq_scr[ci, j], s_prevs[j].astype(BF16)) for j in heads]
        v_news = [u_scr[ci, j] - r1s[j][:c, :] for j in heads]
        r2s = [_dot(ak_scr[ci, j], v_news[j].astype(BF16)) for j in heads]
        for j in heads:
            lanes = slice(j * HEAD_DIM, (j + 1) * HEAD_DIM)
            s_scr[j] = s_prevs[j] * eg_scr[ci, j][0:1, :] + r2s[j][c:, :]
            o = _rms_rows(r1s[j][c:, :] + r2s[j][:c, :], ng_ref[...]) * _silu(z_ref[0, pl.ds(r0, c), lanes])
            y_ref[0, pl.ds(r0, c), lanes] = o.astype(y_ref.dtype)
        return carry_val

    lax.fori_loop(0, n_chunks, apply, 0)
    carry[0] = q_ref[0, lb - 8:, :]
    carry[1] = k_ref[0, lb - 8:, :]
    carry[2] = v_ref[0, lb - 8:, :]
    s_ref[0] = s_scr[...]


def _delta_seg_body(q_ref, k_ref, v_ref, z_ref, gb_ref, wq_ref, wk_ref, wv_ref, ng_ref, s0_ref, y_ref, s_ref):
    c = DN_CHUNK
    seg = SEG_ROWS
    nseg = c // seg
    lb = q_ref.shape[1]
    zeros8 = jnp.zeros((8, HEAD_DIM), F32)
    lane_seg = lax.broadcasted_iota(jnp.int32, (HEAD_DIM, c), 1) // seg
    heads = range(HEAD_GROUP)
    for ci in range(lb // c):
        r0 = ci * c
        parts = _delta_prepare(q_ref, k_ref, v_ref, gb_ref, wq_ref, wk_ref, wv_ref, r0,
                               lambda which, x_ref, lanes: zeros8, seg)
        r1s = []
        for j, (u, w, qe, a_qk, k_dec_t, glast) in enumerate(parts):
            for p in range(nseg):
                rows = slice(p * seg, (p + 1) * seg)
                wq_p = jnp.concatenate([w[rows, :], qe[rows, :]], axis=0).astype(BF16)
                r1s.append(_dot(wq_p, s0_ref[ci * nseg + p, j].astype(BF16)))
        vn16s, o_firsts = [], []
        for j, (u, w, qe, a_qk, k_dec_t, glast) in enumerate(parts):
            mine = r1s[j * nseg:(j + 1) * nseg]
            v_new = u - jnp.concatenate([r1[:seg, :] for r1 in mine], axis=0)
            vn16s.append(v_new.astype(BF16))
            o_firsts.append(jnp.concatenate([r1[seg:, :] for r1 in mine], axis=0))
        o_seconds = [_dot(part[3].astype(BF16), vn16) for part, vn16 in zip(parts, vn16s)]
        for j, (u, w, qe, a_qk, k_dec_t, glast) in enumerate(parts):
            lanes = slice(j * HEAD_DIM, (j + 1) * HEAD_DIM)
            eg_last = jnp.exp(glast)
            for p in range(nseg):
                upd = _dot(jnp.where(lane_seg == p, k_dec_t, 0.0).astype(BF16), vn16s[j])
                s_ref[ci * nseg + p, j] = s0_ref[ci * nseg + p, j] * eg_last[p * seg:p * seg + 1, :] + upd
            o = _rms_rows(o_firsts[j] + o_seconds[j], ng_ref[...]) * _silu(z_ref[0, pl.ds(r0, c), lanes])
            y_ref[0, pl.ds(r0, c), lanes] = o.astype(y_ref.dtype)


def _delta_net(proj3, gates_g, conv_w8, s0, norm_g, lb):
    b, l, _ = proj3.shape
    gw = HEAD_GROUP * HEAD_DIM
    ngrp = HEADS // HEAD_GROUP
    per_d = D_MODEL // gw
    n_chunks = lb // DN_CHUNK

    def col(gi):
        return pl.BlockSpec((1, lb, gw), lambda bi, hi, li, gi=gi: (bi, li, gi * per_d + hi))

    def wcol(gi):
        return pl.BlockSpec((8, gw), lambda bi, hi, li, gi=gi: (0, gi * per_d + hi))

    in_specs = [col(4), col(5), col(6), col(7),
                pl.BlockSpec((1, 1, lb, 128), lambda bi, hi, li: (bi, hi, li, 0)),
                wcol(0), wcol(1), wcol(2),
                pl.BlockSpec((1, HEAD_DIM), lambda bi, hi, li: (0, 0))]
    y_spec = pl.BlockSpec((1, lb, gw), lambda bi, hi, li: (bi, li, hi))
    args = [proj3, proj3, proj3, proj3, gates_g, conv_w8, conv_w8, conv_w8, norm_g]
    if s0 is None:
        state_spec = pl.BlockSpec((1, HEAD_GROUP, HEAD_DIM, HEAD_DIM), lambda bi, hi, li: (bi, hi, 0, 0))
        return pl.pallas_call(
            _delta_seq_body,
            grid=(b, ngrp, l // lb),
            in_specs=in_specs,
            out_specs=[y_spec, state_spec],
            out_shape=[jax.ShapeDtypeStruct((b, l, D_MODEL), BF16),
                       jax.ShapeDtypeStruct((b, HEADS, HEAD_DIM, HEAD_DIM), F32)],
            scratch_shapes=[pltpu.VMEM((HEAD_GROUP, HEAD_DIM, HEAD_DIM), F32),
                            pltpu.VMEM((3, 8, gw), F32),
                            pltpu.VMEM((n_chunks, HEAD_GROUP, DN_CHUNK, HEAD_DIM), F32),
                            pltpu.VMEM((n_chunks, HEAD_GROUP, 2 * DN_CHUNK, HEAD_DIM), BF16),
                            pltpu.VMEM((n_chunks, HEAD_GROUP, DN_CHUNK + HEAD_DIM, DN_CHUNK), BF16),
                            pltpu.VMEM((n_chunks, HEAD_GROUP, 8, HEAD_DIM), F32)],
            compiler_params=_cparams("parallel", "parallel", "arbitrary"),
            name="delta_net_seq",
        )(*args)
    nseq = lb // SEG_ROWS
    state_spec = pl.BlockSpec((nseq, HEAD_GROUP, HEAD_DIM, HEAD_DIM), lambda bi, hi, li: (li, hi, 0, 0))
    return pl.pallas_call(
        _delta_seg_body,
        grid=(b, ngrp, l // lb),
        in_specs=in_specs + [state_spec],
        out_specs=[y_spec, state_spec],
        out_shape=[jax.ShapeDtypeStruct((b, l, D_MODEL), BF16),
                   jax.ShapeDtypeStruct(s0.shape, F32)],
        compiler_params=_cparams("parallel", "parallel", "parallel"),
        name="delta_net_seg",
    )(*args, s0)


def _mla_qkv_body(cq_ref, ckv_ref, pair_ref, cos_ref, sin_ref, qg_ref, kvg_ref, wm_ref, ws_ref, wuk_ref, wuv_ref,
                  q_ref, ckv_out, kpe_out, kf_ref, v_ref, *, expand_kv):
    cos_t = cos_ref[...]
    sin_t = sin_ref[...]
    cqn = _rms_rows(cq_ref[...], qg_ref[...]).astype(BF16)
    ckv = _rms_rows(ckv_ref[...], kvg_ref[...])
    ckv_out[...] = ckv
    pair = pair_ref[...]
    kpe = pair[:, :128] * cos_t + pair[:, 128:] * sin_t
    kpe_out[...] = kpe
    ckv16 = ckv.astype(BF16)
    kpe16 = kpe.astype(BF16)
    for h in range(HEADS):
        qm = _dot(cqn, wm_ref[:, h * QK_PAD:(h + 1) * QK_PAD])
        qs = _dot(cqn, ws_ref[:, h * 128:(h + 1) * 128])
        q_ref[:, h * QK_PAD:h * QK_PAD + 128] = (qm[:, :128] * MLA_SCALE).astype(BF16)
        q_ref[:, h * QK_PAD + 128:(h + 1) * QK_PAD] = ((qm[:, 128:] * cos_t + qs * sin_t) * MLA_SCALE).astype(BF16)
        if expand_kv:
            kf_ref[:, h * QK_PAD:h * QK_PAD + 128] = _dot(ckv16, wuk_ref[:, h * 128:(h + 1) * 128]).astype(BF16)
            kf_ref[:, h * QK_PAD + 128:(h + 1) * QK_PAD] = kpe16
            v_ref[:, h * 128:(h + 1) * 128] = _dot(ckv16, wuv_ref[:, h * 128:(h + 1) * 128]).astype(BF16)
    if not expand_kv:
        kf_ref[...] = jnp.zeros(kf_ref.shape, kf_ref.dtype)
        v_ref[...] = jnp.zeros(v_ref.shape, v_ref.dtype)


def _mla_qkv(proj, cos_t, sin_t, qg, kvg, wm, ws, wuk, wuv, tm, expand_kv):
    t = proj.shape[0]
    kf_shape = (t, HEADS * QK_PAD) if expand_kv else (8, 128)
    v_shape = (t, HEADS * V_DIM) if expand_kv else (8, 128)
    kf_spec = (pl.BlockSpec((tm, HEADS * QK_PAD), lambda i: (i, 0)) if expand_kv
               else pl.BlockSpec((8, 128), lambda i: (0, 0)))
    v_spec = (pl.BlockSpec((tm, HEADS * V_DIM), lambda i: (i, 0)) if expand_kv
              else pl.BlockSpec((8, 128), lambda i: (0, 0)))
    const = lambda shape: pl.BlockSpec(shape, lambda i: (0, 0))
    return pl.pallas_call(
        functools.partial(_mla_qkv_body, expand_kv=expand_kv),
        grid=(t // tm,),
        in_specs=[pl.BlockSpec((tm, Q_LORA), lambda i: (i, 0)),
                  pl.BlockSpec((tm, KV_LORA), lambda i: (i, 1)),
                  pl.BlockSpec((tm, 256), lambda i: (i, PAIR_COL0 // 256)),
                  pl.BlockSpec((tm, 128), lambda i: (i, 0)),
                  pl.BlockSpec((tm, 128), lambda i: (i, 0)),
                  const((1, Q_LORA)), const((1, KV_LORA)),
                  const(wm.shape), const(ws.shape), const(wuk.shape), const(wuv.shape)],
        out_specs=[pl.BlockSpec((tm, HEADS * QK_PAD), lambda i: (i, 0)),
                   pl.BlockSpec((tm, KV_LORA), lambda i: (i, 0)),
                   pl.BlockSpec((tm, 128), lambda i: (i, 0)),
                   kf_spec, v_spec],
        out_shape=[jax.ShapeDtypeStruct((t, HEADS * QK_PAD), BF16),
                   jax.ShapeDtypeStruct((t, KV_LORA), F32),
                   jax.ShapeDtypeStruct((t, 128), F32),
                   jax.ShapeDtypeStruct(kf_shape, BF16),
                   jax.ShapeDtypeStruct(v_shape, BF16)],
        compiler_params=_cparams("arbitrary"),
        name="mla_qkv",
    )(proj, proj, proj, cos_t, sin_t, qg, kvg, wm, ws, wuk, wuv)


def _flash_body(q_ref, k_ref, v_ref, gate_ref, o_ref, m_ref, l_ref, acc_ref, *, tq):
    qi = pl.program_id(2)
    q = q_ref[0]
    m_ref[...] = jnp.full(m_ref.shape, -jnp.inf, F32)
    l_ref[...] = jnp.zeros(l_ref.shape, F32)
    acc_ref[...] = jnp.zeros(acc_ref.shape, F32)

    def step(j, masked):
        r0 = pl.multiple_of(j * tq, tq)
        s = _dot_nt(q, k_ref[0, pl.ds(r0, tq), :])
        if masked:
            ri = lax.broadcasted_iota(jnp.int32, s.shape, 0)
            ci = lax.broadcasted_iota(jnp.int32, s.shape, 1)
            s = jnp.where(ci <= ri, s, -jnp.inf)
        m_prev = m_ref[...]
        m_new = jnp.maximum(m_prev, jnp.max(s, axis=1, keepdims=True))
        alpha = jnp.exp(m_prev - m_new)
        e = jnp.exp(s - m_new[:, :1])
        l_ref[...] = alpha * l_ref[...] + jnp.sum(e, axis=1, keepdims=True)
        acc_ref[...] = alpha * acc_ref[...] + _dot(e.astype(BF16), v_ref[0, pl.ds(r0, tq), :])
        m_ref[...] = m_new

    def loop_body(j, carry_val):
        step(j, False)
        return carry_val

    lax.fori_loop(0, qi, loop_body, 0)
    step(qi, True)
    o_ref[0] = (acc_ref[...] / l_ref[...] * _silu(gate_ref[0])).astype(o_ref.dtype)


def _prompt_attention(q3, k3, v3, proj3, tq):
    b, l, _ = q3.shape
    return pl.pallas_call(
        functools.partial(_flash_body, tq=tq),
        grid=(b, HEADS, l // tq),
        in_specs=[pl.BlockSpec((1, tq, QK_PAD), lambda bi, h, qi: (bi, qi, h)),
                  pl.BlockSpec((1, l, QK_PAD), lambda bi, h, qi: (bi, 0, h)),
                  pl.BlockSpec((1, l, V_DIM), lambda bi, h, qi: (bi, 0, h)),
                  pl.BlockSpec((1, tq, V_DIM), lambda bi, h, qi: (bi, qi, GATE_COL0 // V_DIM + h))],
        out_specs=pl.BlockSpec((1, tq, V_DIM), lambda bi, h, qi: (bi, qi, h)),
        out_shape=jax.ShapeDtypeStruct((b, l, HEADS * V_DIM), BF16),
        scratch_shapes=[pltpu.VMEM((tq, 128), F32), pltpu.VMEM((tq, 128), F32), pltpu.VMEM((tq, V_DIM), F32)],
        compiler_params=_cparams("parallel", "parallel", "arbitrary"),
        name="prompt_attention",
    )(q3, k3, v3, proj3)


def _head_mm_body(x_ref, w_ref, o_ref):
    o_ref[0] = _dot(x_ref[...], w_ref[0]).astype(o_ref.dtype)


def _absorb_queries(q, w_ukt):
    t = q.shape[0]
    return pl.pallas_call(
        _head_mm_body,
        grid=(HEADS,),
        in_specs=[pl.BlockSpec((t, NOPE), lambda h: (0, 2 * h)),
                  pl.BlockSpec((1, NOPE, KV_LORA), lambda h: (h, 0, 0))],
        out_specs=pl.BlockSpec((1, t, KV_LORA), lambda h: (h, 0, 0)),
        out_shape=jax.ShapeDtypeStruct((HEADS, t, KV_LORA), BF16),
        compiler_params=_cparams("parallel"),
        name="absorb_queries",
    )(q, w_ukt)


def _expand_values_body(x_ref, w_ref, gate_ref, o_ref):
    o_ref[...] = (_dot(x_ref[0], w_ref[0]) * _silu(gate_ref[...])).astype(o_ref.dtype)


def _expand_values(o_lat, w_uvh, proj):
    t = o_lat.shape[1]
    return pl.pallas_call(
        _expand_values_body,
        grid=(HEADS,),
        in_specs=[pl.BlockSpec((1, t, KV_LORA), lambda h: (h, 0, 0)),
                  pl.BlockSpec((1, KV_LORA, V_DIM), lambda h: (h, 0, 0)),
                  pl.BlockSpec((t, V_DIM), lambda h: (0, GATE_COL0 // V_DIM + h))],
        out_specs=pl.BlockSpec((t, V_DIM), lambda h: (0, h)),
        out_shape=jax.ShapeDtypeStruct((t, HEADS * V_DIM), BF16),
        compiler_params=_cparams("parallel"),
        name="expand_values",
    )(o_lat, w_uvh, proj)


def _paged_attn_body(pt_ref, ql_ref, qr_ref, cn_ref, rn_ref, *rest, dec_seq):
    npg = PAGES_PER_STEP
    c_refs, r_refs = rest[:npg], rest[npg:2 * npg]
    o_ref, m_ref, l_ref, acc_ref, cbuf, sbuf = rest[2 * npg:]
    j = pl.program_id(1)
    slot = j % 2
    ql = ql_ref[0]
    qr = qr_ref[0]

    def fold(s, values):
        m_prev = m_ref[...]
        m_new = jnp.maximum(m_prev, jnp.max(s, axis=1, keepdims=True))
        alpha = jnp.exp(m_prev - m_new)
        e = jnp.exp(s - m_new[:, :1])
        l_ref[...] = alpha * l_ref[...] + jnp.sum(e, axis=1, keepdims=True)
        acc_ref[...] = alpha[:, :1] * acc_ref[...] + _dot(e.astype(BF16), values)
        m_ref[...] = m_new

    @pl.when(j == 0)
    def _():
        m_ref[...] = jnp.full(m_ref.shape, -jnp.inf, F32)
        l_ref[...] = jnp.zeros(l_ref.shape, F32)
        acc_ref[...] = jnp.zeros(acc_ref.shape, F32)
        c = cn_ref[0].astype(BF16)
        s = _dot_nt(ql, c) + _dot_nt(qr, rn_ref[0].astype(BF16))
        ri = lax.broadcasted_iota(jnp.int32, s.shape, 0)
        ci = lax.broadcasted_iota(jnp.int32, s.shape, 1)
        fold(jnp.where(ci <= ri % dec_seq, s, -jnp.inf), c)
        sbuf[1] = jnp.full(sbuf.shape[1:], -jnp.inf, F32)
        cbuf[1] = jnp.zeros(cbuf.shape[1:], BF16)

    for i, (c_ref, r_ref) in enumerate(zip(c_refs, r_refs)):
        c = c_ref[0, 0].astype(BF16)
        cbuf[slot, i * PAGE_SIZE:(i + 1) * PAGE_SIZE, :] = c
        sbuf[slot, :, i * PAGE_SIZE:(i + 1) * PAGE_SIZE] = _dot_nt(ql, c) + _dot(qr, r_ref[0, 0].astype(BF16))
    fold(sbuf[1 - slot], cbuf[1 - slot])

    @pl.when(j == pl.num_programs(1) - 1)
    def _():
        fold(sbuf[slot], cbuf[slot])
        o_ref[0] = (acc_ref[...] / l_ref[...][:, :1]).astype(o_ref.dtype)


def _paged_attention(q_lat, q_r, ckv_new, kpe_new, cache_ckv, cache_kpe_t, page_table_flat, n_pages, dec_seq):
    b, rows, _ = q_lat.shape
    npg = PAGES_PER_STEP

    def page_spec(shape, i):
        return pl.BlockSpec((1, 1) + shape, lambda bi, j, pt, i=i: (0, pt[bi * n_pages + j * npg + i], 0, 0))

    grid_spec = pltpu.PrefetchScalarGridSpec(
        num_scalar_prefetch=1,
        grid=(b, n_pages // npg),
        in_specs=([pl.BlockSpec((1, rows, KV_LORA), lambda bi, j, pt: (bi, 0, 0)),
                   pl.BlockSpec((1, rows, ROPE), lambda bi, j, pt: (bi, 0, 0)),
                   pl.BlockSpec((1, PAGE_SIZE, KV_LORA), lambda bi, j, pt: (bi, 0, 0)),
                   pl.BlockSpec((1, PAGE_SIZE, ROPE), lambda bi, j, pt: (bi, 0, 0))]
                  + [page_spec((PAGE_SIZE, KV_LORA), i) for i in range(npg)]
                  + [page_spec((ROPE, PAGE_SIZE), i) for i in range(npg)]),
        out_specs=pl.BlockSpec((1, rows, KV_LORA), lambda bi, j, pt: (bi, 0, 0)),
        scratch_shapes=[pltpu.VMEM((rows, 128), F32), pltpu.VMEM((rows, 128), F32),
                        pltpu.VMEM((rows, KV_LORA), F32),
                        pltpu.VMEM((2, npg * PAGE_SIZE, KV_LORA), BF16),
                        pltpu.VMEM((2, rows, npg * PAGE_SIZE), F32)],
    )
    return pl.pallas_call(
        functools.partial(_paged_attn_body, dec_seq=dec_seq),
        grid_spec=grid_spec,
        out_shape=jax.ShapeDtypeStruct((b, rows, KV_LORA), BF16),
        compiler_params=_cparams("parallel", "arbitrary"),
        name="paged_attention",
    )(page_table_flat, q_lat, q_r, ckv_new, kpe_new, *([cache_ckv] * npg), *([cache_kpe_t] * npg))


def _prepare_weights(norm_g, ple_norm_g, w_ple_gate, w_ple_proj, final_norm_g, w_in_even, w_out_even, conv_a_w,
                     conv_qkv_w, dn_a_log, dn_dt_bias, dn_norm_g, w_in_mla, mla_q_norm_g, mla_kv_norm_g, w_uq,
                     w_uk, w_uv, w_o_mla):
    d = D_MODEL
    w = {}
    w["norm_g"] = norm_g.reshape(2, 1, d)
    w["ple_norm_g"] = ple_norm_g.reshape(2, 1, d)
    w["final_norm_g"] = final_norm_g.reshape(1, d)
    w["w_ple_gate"] = w_ple_gate.astype(BF16)
    w["w_ple_proj"] = w_ple_proj.astype(BF16)
    w_in = w_in_even[0]
    w["w_in_main"] = w_in[:, :8 * d].astype(BF16)
    w["w_in_gates"] = jnp.pad(w_in[:, 8 * d:], ((0, 0), (0, 128 - 2 * HEADS))).astype(BF16)
    w["a_log_row"] = jnp.pad(dn_a_log[0], (HEADS, 128 - 2 * HEADS)).reshape(1, 128)
    w["dt_bias_row"] = jnp.pad(dn_dt_bias[0], (HEADS, 128 - 2 * HEADS)).reshape(1, 128)
    w["w_out_a"] = w_out_even[0, :d].astype(BF16)
    w["w_out_b"] = w_out_even[0, d:].astype(BF16)
    w["conv_a_w8"] = jnp.pad(conv_a_w[0], ((0, 8 - SC_CONV), (0, 0)))
    w["conv_qkv_w8"] = jnp.pad(conv_qkv_w[0], ((0, 8 - DN_CONV), (0, 0)))
    w["dn_norm_g"] = dn_norm_g[0].reshape(1, HEAD_DIM)
    wi = w_in_mla[0]
    k_pe = wi[:, 2 * Q_LORA:2 * Q_LORA + ROPE]
    half = ROPE // 2
    zeros64 = jnp.zeros((d, 128 - ROPE), F32)
    w["w_in_mla"] = jnp.concatenate(
        [wi[:, :2 * Q_LORA], wi[:, 2 * Q_LORA + ROPE:], k_pe, zeros64, k_pe[:, half:], k_pe[:, :half], zeros64,
         jnp.zeros((d, MLA_IN_PAD - PAIR_COL0 - 256), F32)], axis=1).astype(BF16)
    w["mla_q_norm_g"] = mla_q_norm_g[0].reshape(1, Q_LORA)
    w["mla_kv_norm_g"] = mla_kv_norm_g[0].reshape(1, KV_LORA)
    uq = w_uq[0].reshape(Q_LORA, HEADS, NOPE + ROPE)
    x1, x2 = uq[:, :, NOPE:NOPE + half], uq[:, :, NOPE + half:]
    zpad = jnp.zeros((Q_LORA, HEADS, QK_PAD - NOPE - ROPE), F32)
    w["w_uq_main"] = jnp.concatenate([uq, zpad], axis=2).reshape(Q_LORA, HEADS * QK_PAD).astype(BF16)
    w["w_uq_swap"] = jnp.concatenate([x2, x1, zpad], axis=2).reshape(Q_LORA, HEADS * 128).astype(BF16)
    w["w_uk"] = w_uk[0].reshape(KV_LORA, HEADS * NOPE).astype(BF16)
    w["w_uv"] = w_uv[0].reshape(KV_LORA, HEADS * V_DIM).astype(BF16)
    w["w_uk_t"] = jnp.transpose(w_uk[0], (1, 2, 0)).astype(BF16)
    w["w_uv_h"] = jnp.transpose(w_uv[0], (1, 0, 2)).astype(BF16)
    w["w_o_mla"] = w_o_mla[0].astype(BF16)
    return w


def _rope_tables(pos):
    half = ROPE // 2
    inv = ROPE_THETA ** (-jnp.arange(half, dtype=F32) / half)
    ang = pos.astype(F32)[:, None] * inv
    cos, sin = jnp.cos(ang), jnp.sin(ang)
    z = jnp.zeros((pos.shape[0], 128 - ROPE), F32)
    return jnp.concatenate([cos, cos, z], axis=1), jnp.concatenate([-sin, sin, z], axis=1)


def _group_gates(gates3):
    b, l, _ = gates3.shape
    ngrp = HEADS // HEAD_GROUP
    beta = gates3[:, :, :HEADS].reshape(b, l, ngrp, HEAD_GROUP)
    g = gates3[:, :, HEADS:2 * HEADS].reshape(b, l, ngrp, HEAD_GROUP)
    grouped = jnp.transpose(jnp.concatenate([beta, g], axis=-1), (0, 2, 1, 3))
    return jnp.pad(grouped, ((0, 0), (0, 0), (0, 0), (0, 128 - 2 * HEAD_GROUP)))


def _layer_stack(x, p, pos, conv_a0, conv_qkv0, delta0, past, w, tm):
    b, l, d = x.shape
    t = b * l
    h = x.reshape(t, d)

    proj = _rms_matmul(h, w["norm_g"][0], w["w_in_main"], F32, tm, 1024, "in_proj_even")
    gates = _rms_matmul_gates(h, w["norm_g"][0], w["w_in_gates"], w["a_log_row"], w["dt_bias_row"], tm)
    proj3 = proj.reshape(b, l, 8 * d)
    gates3 = gates.reshape(b, l, 128)
    conv_qkv1 = proj3[:, l - (DN_CONV - 1):l, 4 * d:7 * d]
    if past is None:
        y_a, a_last8 = _mixer_a(proj3, w["conv_a_w8"], 512, 512, False)
        y_b, delta1 = _delta_net(proj3, _group_gates(gates3), w["conv_qkv_w8"], None, w["dn_norm_g"], 256)
        conv_a1 = a_last8[:, 8 - (SC_CONV - 1):]
    else:
        n_state = DN_CONV - 1
        tail = SEG_ROWS - n_state - l
        a_state = jnp.pad(conv_a0, ((0, 0), (n_state - (SC_CONV - 1), 0), (0, 0)))
        a_ones = jnp.pad(jnp.ones((b, SC_CONV - 1, d), F32), ((0, 0), (n_state - (SC_CONV - 1), 0), (0, 0)))
        state_rows = jnp.concatenate([a_state, a_ones, jnp.zeros((b, n_state, 2 * d), F32), conv_qkv0,
                                      jnp.zeros((b, n_state, d), F32)], axis=2)
        stacked = jnp.concatenate([state_rows, proj3, jnp.zeros((b, tail, 8 * d), F32)], axis=1)
        stacked = stacked.reshape(1, b * SEG_ROWS, 8 * d)
        gates_s = jnp.pad(gates3, ((0, 0), (n_state, tail), (0, 0))).reshape(1, b * SEG_ROWS, 128)
        tl = min(b * SEG_ROWS, 512)
        y_a, prod = _mixer_a(stacked, w["conv_a_w8"], tl, 512, True)
        y_b, delta1 = _delta_net(stacked, _group_gates(gates_s), w["conv_qkv_w8"], delta0, w["dn_norm_g"], DN_CHUNK)
        tok = slice(n_state, n_state + l)
        y_a = y_a.reshape(b, SEG_ROWS, d)[:, tok]
        y_b = y_b.reshape(b, SEG_ROWS, d)[:, tok]
        conv_a1 = prod.reshape(b, SEG_ROWS, d)[:, n_state + l - (SC_CONV - 1):n_state + l]
    h = _matmul_residual([y_a.reshape(t, d), y_b.reshape(t, d)], [w["w_out_a"], w["w_out_b"]], h, min(tm, 512),
                         1024, "out_proj_even")
    h = _ple(h, w["ple_norm_g"][0], w["w_ple_gate"][0], p[0].reshape(t, PLE_DIM), w["w_ple_proj"][0],
             w["final_norm_g"], False, min(tm, 512), "ple0")

    proj = _rms_matmul(h, w["norm_g"][1], w["w_in_mla"], F32, tm, 512, "in_proj_mla")
    cos_t, sin_t = _rope_tables(pos)
    cos_t = jnp.broadcast_to(cos_t[None], (b, l, 128)).reshape(t, 128)
    sin_t = jnp.broadcast_to(sin_t[None], (b, l, 128)).reshape(t, 128)
    q, ckv, kpe128, kfull, v = _mla_qkv(proj, cos_t, sin_t, w["mla_q_norm_g"], w["mla_kv_norm_g"], w["w_uq_main"],
                                        w["w_uq_swap"], w["w_uk"], w["w_uv"], min(t, 256), past is None)
    kpe = kpe128[:, :ROPE]
    if past is None:
        o = _prompt_attention(q.reshape(b, l, HEADS * QK_PAD), kfull.reshape(b, l, HEADS * QK_PAD),
                              v.reshape(b, l, HEADS * V_DIM), proj.reshape(b, l, MLA_IN_PAD), 512)
        o = o.reshape(t, HEADS * V_DIM)
    else:
        cache_ckv, cache_kpe, page_table = past
        n_pages = page_table.shape[1]
        q_lat = _absorb_queries(q, w["w_uk_t"])
        q_lat = jnp.transpose(q_lat.reshape(HEADS, b, l, KV_LORA), (1, 0, 2, 3)).reshape(b, HEADS * l, KV_LORA)
        q_r = q.reshape(b, l, HEADS, QK_PAD)[:, :, :, NOPE:NOPE + ROPE]
        q_r = jnp.transpose(q_r, (0, 2, 1, 3)).reshape(b, HEADS * l, ROPE)
        ckv_new = jnp.pad(ckv.reshape(b, l, KV_LORA), ((0, 0), (0, PAGE_SIZE - l), (0, 0)))
        kpe_new = jnp.pad(kpe.reshape(b, l, ROPE), ((0, 0), (0, PAGE_SIZE - l), (0, 0)))
        o_lat = _paged_attention(q_lat, q_r, ckv_new, kpe_new, cache_ckv, jnp.swapaxes(cache_kpe, 2, 3),
                                 page_table.reshape(-1), n_pages, l)
        o_lat = jnp.transpose(o_lat.reshape(b, HEADS, l, KV_LORA), (1, 0, 2, 3)).reshape(HEADS, t, KV_LORA)
        o = _expand_values(o_lat, w["w_uv_h"], proj)
    h = _matmul_residual([o], [w["w_o_mla"]], h, min(tm, 512), 1024, "out_proj_mla")
    y = _ple(h, w["ple_norm_g"][1], w["w_ple_gate"][1], p[1].reshape(t, PLE_DIM), w["w_ple_proj"][1],
             w["final_norm_g"], True, min(tm, 512), "ple1")
    return (y.reshape(b, l, d), conv_a1[None], conv_qkv1[None], delta1[None],
            ckv.reshape(1, b, l, KV_LORA), kpe.reshape(1, b, l, ROPE))


def kernel(x_prompt, x_sample, cache_ckv, cache_kpe, state_conv_a, state_conv_qkv, state_delta, page_table,
           p_prompt, p_sample, norm_g, ple_norm_g, w_ple_gate, w_ple_proj, final_norm_g, w_in_even, w_out_even,
           conv_a_w, conv_qkv_w, dn_a_log, dn_dt_bias, dn_norm_g, w_in_mla, mla_q_norm_g, mla_kv_norm_g, w_uq,
           w_uk, w_uv, w_o_mla):
    w = _prepare_weights(norm_g, ple_norm_g, w_ple_gate, w_ple_proj, final_norm_g, w_in_even, w_out_even, conv_a_w,
                         conv_qkv_w, dn_a_log, dn_dt_bias, dn_norm_g, w_in_mla, mla_q_norm_g, mla_kv_norm_g, w_uq,
                         w_uk, w_uv, w_o_mla)
    seq = x_prompt.shape[1]
    dec_b, dec_l, _ = x_sample.shape
    past_len = page_table.shape[1] * PAGE_SIZE
    outs_p = _layer_stack(x_prompt, p_prompt, jnp.arange(seq), None, None, None, None, w, 1024)
    outs_s = _layer_stack(x_sample, p_sample, past_len + jnp.arange(dec_l), state_conv_a[0], state_conv_qkv[0],
                          state_delta[0], (cache_ckv, cache_kpe, page_table), w, dec_b * dec_l)
    return (outs_p[0], outs_s[0]) + tuple(outs_p[1:]) + tuple(outs_s[1:])
```

```python
import functools
import math

import jax
import jax.numpy as jnp
from jax import lax
from jax.experimental import pallas as pl
from jax.experimental.pallas import tpu as pltpu

F32 = jnp.float32
BF16 = jnp.bfloat16

NORM_EPS = 1e-6
D_MODEL = 2048
HEADS = 16
HEAD_DIM = 128
DN_CHUNK = 64
DN_CONV = 4
SC_CONV = 3
Q_LORA = 512
KV_LORA = 512
NOPE = 128
ROPE = 64
V_DIM = 128
PLE_DIM = 256
ROPE_THETA = 10000.0
PAGE_SIZE = 128
MLA_SCALE = (NOPE + ROPE) ** -0.5
QK_PAD = 256
MLA_IN_PAD = 3584
GATE_COL0 = Q_LORA + KV_LORA
PAIR_COL0 = GATE_COL0 + HEADS * V_DIM
HEAD_GROUP = 8
SEG_ROWS = 8
PAGES_PER_STEP = 16

V7X_VMEM_LIMIT_BYTES = 56 * 1024 * 1024


def _cparams(*sem):
    return pltpu.CompilerParams(dimension_semantics=sem, vmem_limit_bytes=V7X_VMEM_LIMIT_BYTES)


def _sigmoid(x):
    return 1.0 / (1.0 + jnp.exp(-x))


def _silu(x):
    return x * _sigmoid(x)


def _rms_rows(x, g):
    return x * lax.rsqrt(jnp.mean(x * x, axis=-1, keepdims=True) + NORM_EPS) * g


def _dot(a, b):
    return jnp.dot(a, b, preferred_element_type=F32)


def _dot_nt(a, b):
    return lax.dot_general(a, b, (((1,), (1,)), ((), ())), preferred_element_type=F32)


def _rms_mm_body(x_ref, g_ref, w_ref, o_ref, xn_ref):
    @pl.when(pl.program_id(1) == 0)
    def _():
        xn_ref[...] = _rms_rows(x_ref[...], g_ref[...]).astype(BF16)

    o_ref[...] = _dot(xn_ref[...], w_ref[...]).astype(o_ref.dtype)


def _rms_matmul(x, g, w, out_dtype, tm, tn, name):
    t, k = x.shape
    n = w.shape[1]
    return pl.pallas_call(
        _rms_mm_body,
        grid=(t // tm, n // tn),
        in_specs=[pl.BlockSpec((tm, k), lambda i, j: (i, 0)),
                  pl.BlockSpec((1, k), lambda i, j: (0, 0)),
                  pl.BlockSpec((k, tn), lambda i, j: (0, j))],
        out_specs=pl.BlockSpec((tm, tn), lambda i, j: (i, j)),
        out_shape=jax.ShapeDtypeStruct((t, n), out_dtype),
        scratch_shapes=[pltpu.VMEM((tm, k), BF16)],
        compiler_params=_cparams("parallel", "arbitrary"),
        name=name,
    )(x, g, w)


def _rms_mm_gates_body(x_ref, g_ref, w_ref, alog_ref, dtb_ref, o_ref):
    xn = _rms_rows(x_ref[...], g_ref[...]).astype(BF16)
    y = _dot(xn, w_ref[...])
    lane = lax.broadcasted_iota(jnp.int32, y.shape, 1)
    beta = _sigmoid(y)
    t = y + dtb_ref[...]
    softplus = jnp.maximum(t, 0.0) + jnp.log1p(jnp.exp(-jnp.abs(t)))
    decay = -jnp.exp(alog_ref[...]) * softplus
    o_ref[...] = jnp.where(lane < HEADS, beta, decay)


def _rms_matmul_gates(x, g, w, alog_row, dtb_row, tm):
    t, k = x.shape
    return pl.pallas_call(
        _rms_mm_gates_body,
        grid=(t // tm,),
        in_specs=[pl.BlockSpec((tm, k), lambda i: (i, 0)),
                  pl.BlockSpec((1, k), lambda i: (0, 0)),
                  pl.BlockSpec((k, 128), lambda i: (0, 0)),
                  pl.BlockSpec((1, 128), lambda i: (0, 0)),
                  pl.BlockSpec((1, 128), lambda i: (0, 0))],
        out_specs=pl.BlockSpec((tm, 128), lambda i: (i, 0)),
        out_shape=jax.ShapeDtypeStruct((t, 128), F32),
        compiler_params=_cparams("parallel"),
        name="dn_gates",
    )(x, g, w, alog_row, dtb_row)


def _mm_res_body(*refs, n_in):
    xs, ws, res_ref, o_ref = refs[:n_in], refs[n_in:2 * n_in], refs[2 * n_in], refs[2 * n_in + 1]
    acc = res_ref[...]
    for x_ref, w_ref in zip(xs, ws):
        acc = acc + _dot(x_ref[...], w_ref[...])
    o_ref[...] = acc


def _matmul_residual(xs, ws, res, tm, tn, name):
    t, n = res.shape
    n_in = len(xs)
    in_specs = ([pl.BlockSpec((tm, x.shape[1]), lambda i, j: (i, 0)) for x in xs]
                + [pl.BlockSpec((w.shape[0], tn), lambda i, j: (0, j)) for w in ws]
                + [pl.BlockSpec((tm, tn), lambda i, j: (i, j))])
    return pl.pallas_call(
        functools.partial(_mm_res_body, n_in=n_in),
        grid=(t // tm, n // tn),
        in_specs=in_specs,
        out_specs=pl.BlockSpec((tm, tn), lambda i, j: (i, j)),
        out_shape=jax.ShapeDtypeStruct((t, n), F32),
        compiler_params=_cparams("parallel", "arbitrary"),
        name=name,
    )(*xs, *ws, res)


def _ple_body(h_ref, g_ref, wg_ref, p_ref, wp_ref, fg_ref, o_ref, *, final_norm):
    h = h_ref[...]
    xn = _rms_rows(h, g_ref[...]).astype(BF16)
    gate = _sigmoid(_dot(xn, wg_ref[...]))
    h2 = h + gate * _dot(p_ref[...].astype(BF16), wp_ref[...])
    if final_norm:
        h2 = _rms_rows(h2, fg_ref[...])
    o_ref[...] = h2


def _ple(h, g, w_gate, p, w_proj, final_g, final_norm, tm, name):
    t, d = h.shape
    pd = p.shape[1]
    return pl.pallas_call(
        functools.partial(_ple_body, final_norm=final_norm),
        grid=(t // tm,),
        in_specs=[pl.BlockSpec((tm, d), lambda i: (i, 0)),
                  pl.BlockSpec((1, d), lambda i: (0, 0)),
                  pl.BlockSpec((d, d), lambda i: (0, 0)),
                  pl.BlockSpec((tm, pd), lambda i: (i, 0)),
                  pl.BlockSpec((pd, d), lambda i: (0, 0)),
                  pl.BlockSpec((1, d), lambda i: (0, 0))],
        out_specs=pl.BlockSpec((tm, d), lambda i: (i, 0)),
        out_shape=jax.ShapeDtypeStruct((t, d), F32),
        compiler_params=_cparams("parallel"),
        name=name,
    )(h, g, w_gate, p, w_proj, final_g)


def _mixer_a_body(ac_ref, ah_ref, ab_ref, az_ref, w_ref, y_ref, st_ref, carry_ref, *, emit_prod):
    @pl.when(pl.program_id(2) == 0)
    def _():
        carry_ref[...] = jnp.zeros(carry_ref.shape, F32)

    prod = ac_ref[0] * ah_ref[0]
    tl = prod.shape[0]
    row = lax.broadcasted_iota(jnp.int32, prod.shape, 0)
    c6 = carry_ref[6:7, :]
    c7 = carry_ref[7:8, :]
    p1 = jnp.where(row == 0, c7, pltpu.roll(prod, 1, axis=0))
    p2 = jnp.where(row == 0, c6, jnp.where(row == 1, c7, pltpu.roll(prod, 2, axis=0)))
    w = w_ref[...]
    u = w[2:3, :] * prod + w[1:2, :] * p1 + w[0:1, :] * p2
    y_ref[0] = (ab_ref[0] * u * _silu(az_ref[0])).astype(y_ref.dtype)
    last = prod[tl - 8:, :]
    carry_ref[...] = last
    st_ref[0] = prod if emit_prod else last


def _mixer_a(proj3, conv_w8, tl, tc, emit_prod):
    b, l, _ = proj3.shape
    nct = D_MODEL // tc
    st_rows = tl if emit_prod else 8

    def col(gi):
        return pl.BlockSpec((1, tl, tc), lambda bi, ci, li, gi=gi: (bi, li, gi * nct + ci))

    return pl.pallas_call(
        functools.partial(_mixer_a_body, emit_prod=emit_prod),
        grid=(b, nct, l // tl),
        in_specs=[col(0), col(1), col(2), col(3),
                  pl.BlockSpec((8, tc), lambda bi, ci, li: (0, ci))],
        out_specs=[pl.BlockSpec((1, tl, tc), lambda bi, ci, li: (bi, li, ci)),
                   pl.BlockSpec((1, st_rows, tc), lambda bi, ci, li: (bi, li if emit_prod else 0, ci))],
        out_shape=[jax.ShapeDtypeStruct((b, l, D_MODEL), BF16),
                   jax.ShapeDtypeStruct((b, l if emit_prod else 8, D_MODEL), F32)],
        scratch_shapes=[pltpu.VMEM((8, tc), F32)],
        compiler_params=_cparams("parallel", "parallel", "arbitrary"),
        name="mixer_a",
    )(proj3, proj3, proj3, proj3, conv_w8)


def _segment_scan(x, seg):
    n = x.shape[0]
    row = lax.broadcasted_iota(jnp.int32, x.shape, 0)
    pos = row % seg
    cs = x
    s = 1
    while s < seg:
        cs = cs + jnp.where(pos >= s, pltpu.roll(cs, s, axis=0), 0.0)
        s *= 2
    if seg == n:
        return cs, jnp.broadcast_to(cs[n - 1:n, :], x.shape)
    tot = cs
    s = 1
    while s < seg:
        tot = jnp.where(pos + s < seg, pltpu.roll(tot, n - s, axis=0), tot)
        s *= 2
    return cs, tot


def _delta_prepare(q_ref, k_ref, v_ref, gb_ref, wq_ref, wk_ref, wv_ref, r0s, prev_rows, seg):
    c = DN_CHUNK
    items = [(n, r0, j) for n, r0 in enumerate(r0s) for j in range(HEAD_GROUP)]
    ri = lax.broadcasted_iota(jnp.int32, (c, c), 0)
    ci = lax.broadcasted_iota(jnp.int32, (c, c), 1)
    bcols, gcols, glasts = [], [], []
    for r0 in r0s:
        gates = gb_ref[0, 0, pl.ds(r0, c), :]
        gsum, gtot = _segment_scan(gates, seg)
        for j in range(HEAD_GROUP):
            bcols.append(jnp.broadcast_to(gates[:, j:j + 1], (c, HEAD_DIM)))
            gcols.append(jnp.broadcast_to(gsum[:, HEAD_GROUP + j:HEAD_GROUP + j + 1], (c, HEAD_DIM)))
            glasts.append(jnp.broadcast_to(gtot[:, HEAD_GROUP + j:HEAD_GROUP + j + 1], (c, HEAD_DIM)))

    def conv_silu(which, x_ref, w_ref, n, r0, j):
        lanes = slice(j * HEAD_DIM, (j + 1) * HEAD_DIM)
        cur = x_ref[0, pl.ds(r0, c), lanes]
        ext = jnp.concatenate([prev_rows(which, x_ref, lanes, n), cur], axis=0)
        w = w_ref[:, lanes]
        y = (w[3:4, :] * cur + w[2:3, :] * ext[7:7 + c, :] + w[1:2, :] * ext[6:6 + c, :]
             + w[0:1, :] * ext[5:5 + c, :])
        return _silu(y)

    qs = [conv_silu(0, q_ref, wq_ref, *item) for item in items]
    ks = [conv_silu(1, k_ref, wk_ref, *item) for item in items]
    vs = [conv_silu(2, v_ref, wv_ref, *item) for item in items]
    qs = [q * lax.rsqrt(jnp.sum(q * q, axis=-1, keepdims=True) + NORM_EPS) * (HEAD_DIM ** -0.5) for q in qs]
    ks = [k * lax.rsqrt(jnp.sum(k * k, axis=-1, keepdims=True) + NORM_EPS) for k in ks]
    kbs = [k * b for k, b in zip(ks, bcols)]
    m1s = [_dot_nt(jnp.concatenate([kb, q], axis=0).astype(BF16), k.astype(BF16))
           for kb, q, k in zip(kbs, qs, ks)]
    decays = []
    for gcol in gcols:
        grow = gcol.T[:c, :]
        decay = jnp.where(ri >= ci, jnp.exp(jnp.minimum(gcol[:, :c] - grow, 0.0)), 0.0)
        if seg < c:
            decay = jnp.where(ri // seg == ci // seg, decay, 0.0)
        decays.append(decay)
    powers = [jnp.where(ri > ci, m1[:c, :] * d, 0.0) for m1, d in zip(m1s, decays)]
    a_qks = [m1[c:, :] * d for m1, d in zip(m1s, decays)]
    t_invs = [jnp.where(ri == ci, 1.0, 0.0) - a for a in powers]
    span = 2
    while span < seg:
        p16s = [p.astype(BF16) for p in powers]
        powers = [_dot(p16, p16) for p16 in p16s]
        t_invs = [t + _dot(t.astype(BF16), p.astype(BF16)) for t, p in zip(t_invs, powers)]
        span *= 2
    egs = [jnp.exp(g) for g in gcols]
    sols = [_dot(t.astype(BF16), jnp.concatenate([v * b, kb * eg], axis=1).astype(BF16))
            for t, v, b, kb, eg in zip(t_invs, vs, bcols, kbs, egs)]
    k_dec_ts = [(k * jnp.exp(gl - g)).T for k, gl, g in zip(ks, glasts, gcols)]
    flat = [(sol[:, :HEAD_DIM], sol[:, HEAD_DIM:], q * eg, a_qk, k_dec_t, gl)
            for sol, q, eg, a_qk, k_dec_t, gl in zip(sols, qs, egs, a_qks, k_dec_ts, glasts)]
    return [flat[n * HEAD_GROUP:(n + 1) * HEAD_GROUP] for n in range(len(r0s))]


def _delta_seq_body(q_ref, k_ref, v_ref, z_ref, gb_ref, wq_ref, wk_ref, wv_ref, ng_ref, y_ref, s_ref,
                    s_scr, carry, u_scr, wq_scr, ak_scr, eg_scr):
    c = DN_CHUNK
    lb = q_ref.shape[1]
    n_chunks = lb // c

    @pl.when(pl.program_id(2) == 0)
    def _():
        s_scr[...] = jnp.zeros(s_scr.shape, F32)
        carry[...] = jnp.zeros(carry.shape, F32)

    heads = range(HEAD_GROUP)

    group = 2 if n_chunks % 2 == 0 else 1

    def prepare(gi, carry_val):
        r0s = [pl.multiple_of((gi * group + n) * c, c) for n in range(group)]

        def prev_rows(which, x_ref, lanes, n):
            if n > 0:
                return x_ref[0, pl.ds(pl.multiple_of(r0s[n] - 8, 8), 8), lanes]
            p0 = pl.multiple_of(jnp.maximum(r0s[0] - 8, 0), 8)
            return jnp.where(gi == 0, carry[which, :, lanes], x_ref[0, pl.ds(p0, 8), lanes])

        parts = _delta_prepare(q_ref, k_ref, v_ref, gb_ref, wq_ref, wk_ref, wv_ref, r0s, prev_rows, c)
        for n in range(group):
            ci = gi * group + n
            for j, (u, w, qe, a_qk, k_dec_t, glast) in enumerate(parts[n]):
                u_scr[ci, j] = u
                wq_scr[ci, j] = jnp.concatenate([w, qe], axis=0).astype(BF16)
                ak_scr[ci, j] = jnp.concatenate([a_qk, k_dec_t], axis=0).astype(BF16)
                eg_scr[ci, j] = jnp.exp(glast[:8, :])
        return carry_val

    lax.fori_loop(0, n_chunks // group, prepare, 0)

    def apply(ci, carry_val):
        r0 = pl.multiple_of(ci * c, c)
        s_prevs = [s_scr[j] for j in heads]
        r1s = [_dot(wq_scr[ci, j], s_prevs[j].astype(BF16)) for j in heads]
        v_news = [u_scr[ci, j] - r1s[j][:c, :] for j in heads]
        r2s = [_dot(ak_scr[ci, j], v_news[j].astype(BF16)) for j in heads]
        for j in heads:
            lanes = slice(j * HEAD_DIM, (j + 1) * HEAD_DIM)
            s_scr[j] = s_prevs[j] * eg_scr[ci, j][0:1, :] + r2s[j][c:, :]
            o = _rms_rows(r1s[j][c:, :] + r2s[j][:c, :], ng_ref[...]) * _silu(z_ref[0, pl.ds(r0, c), lanes])
            y_ref[0, pl.ds(r0, c), lanes] = o.astype(y_ref.dtype)
        return carry_val

    lax.fori_loop(0, n_chunks, apply, 0)
    carry[0] = q_ref[0, lb - 8:, :]
    carry[1] = k_ref[0, lb - 8:, :]
    carry[2] = v_ref[0, lb - 8:, :]
    s_ref[0] = s_scr[...]


def _delta_seg_body(q_ref, k_ref, v_ref, z_ref, gb_ref, wq_ref, wk_ref, wv_ref, ng_ref, s0_ref, y_ref, s_ref):
    c = DN_CHUNK
    seg = SEG_ROWS
    nseg = c // seg
    lb = q_ref.shape[1]
    zeros8 = jnp.zeros((8, HEAD_DIM), F32)
    lane_seg = lax.broadcasted_iota(jnp.int32, (HEAD_DIM, c), 1) // seg
    heads = range(HEAD_GROUP)
    for ci in range(lb // c):
        r0 = ci * c
        parts = _delta_prepare(q_ref, k_ref, v_ref, gb_ref, wq_ref, wk_ref, wv_ref, [r0],
                               lambda which, x_ref, lanes, n: zeros8, seg)[0]
        r1s = []
        for j, (u, w, qe, a_qk, k_dec_t, glast) in enumerate(parts):
            for p in range(nseg):
                rows = slice(p * seg, (p + 1) * seg)
                wq_p = jnp.concatenate([w[rows, :], qe[rows, :]], axis=0).astype(BF16)
                r1s.append(_dot(wq_p, s0_ref[ci * nseg + p, j].astype(BF16)))
        vn16s, o_firsts = [], []
        for j, (u, w, qe, a_qk, k_dec_t, glast) in enumerate(parts):
            mine = r1s[j * nseg:(j + 1) * nseg]
            v_new = u - jnp.concatenate([r1[:seg, :] for r1 in mine], axis=0)
            vn16s.append(v_new.astype(BF16))
            o_firsts.append(jnp.concatenate([r1[seg:, :] for r1 in mine], axis=0))
        o_seconds = [_dot(part[3].astype(BF16), vn16) for part, vn16 in zip(parts, vn16s)]
        for j, (u, w, qe, a_qk, k_dec_t, glast) in enumerate(parts):
            lanes = slice(j * HEAD_DIM, (j + 1) * HEAD_DIM)
            eg_last = jnp.exp(glast)
            for p in range(nseg):
                upd = _dot(jnp.where(lane_seg == p, k_dec_t, 0.0).astype(BF16), vn16s[j])
                s_ref[ci * nseg + p, j] = s0_ref[ci * nseg + p, j] * eg_last[p * seg:p * seg + 1, :] + upd
            o = _rms_rows(o_firsts[j] + o_seconds[j], ng_ref[...]) * _silu(z_ref[0, pl.ds(r0, c), lanes])
            y_ref[0, pl.ds(r0, c), lanes] = o.astype(y_ref.dtype)


def _delta_net(proj3, gates_g, conv_w8, s0, norm_g, lb):
    b, l, _ = proj3.shape
    gw = HEAD_GROUP * HEAD_DIM
    ngrp = HEADS // HEAD_GROUP
    per_d = D_MODEL // gw
    n_chunks = lb // DN_CHUNK

    def col(gi):
        return pl.BlockSpec((1, lb, gw), lambda bi, hi, li, gi=gi: (bi, li, gi * per_d + hi))

    def wcol(gi):
        return pl.BlockSpec((8, gw), lambda bi, hi, li, gi=gi: (0, gi * per_d + hi))

    in_specs = [col(4), col(5), col(6), col(7),
                pl.BlockSpec((1, 1, lb, 128), lambda bi, hi, li: (bi, hi, li, 0)),
                wcol(0), wcol(1), wcol(2),
                pl.BlockSpec((1, HEAD_DIM), lambda bi, hi, li: (0, 0))]
    y_spec = pl.BlockSpec((1, lb, gw), lambda bi, hi, li: (bi, li, hi))
    args = [proj3, proj3, proj3, proj3, gates_g, conv_w8, conv_w8, conv_w8, norm_g]
    if s0 is None:
        state_spec = pl.BlockSpec((1, HEAD_GROUP, HEAD_DIM, HEAD_DIM), lambda bi, hi, li: (bi, hi, 0, 0))
        return pl.pallas_call(
            _delta_seq_body,
            grid=(b, ngrp, l // lb),
            in_specs=in_specs,
            out_specs=[y_spec, state_spec],
            out_shape=[jax.ShapeDtypeStruct((b, l, D_MODEL), BF16),
                       jax.ShapeDtypeStruct((b, HEADS, HEAD_DIM, HEAD_DIM), F32)],
            scratch_shapes=[pltpu.VMEM((HEAD_GROUP, HEAD_DIM, HEAD_DIM), F32),
                            pltpu.VMEM((3, 8, gw), F32),
                            pltpu.VMEM((n_chunks, HEAD_GROUP, DN_CHUNK, HEAD_DIM), F32),
                            pltpu.VMEM((n_chunks, HEAD_GROUP, 2 * DN_CHUNK, HEAD_DIM), BF16),
                            pltpu.VMEM((n_chunks, HEAD_GROUP, DN_CHUNK + HEAD_DIM, DN_CHUNK), BF16),
                            pltpu.VMEM((n_chunks, HEAD_GROUP, 8, HEAD_DIM), F32)],
            compiler_params=_cparams("parallel", "parallel", "arbitrary"),
            name="delta_net_seq",
        )(*args)
    nseq = lb // SEG_ROWS
    state_spec = pl.BlockSpec((nseq, HEAD_GROUP, HEAD_DIM, HEAD_DIM), lambda bi, hi, li: (li, hi, 0, 0))
    return pl.pallas_call(
        _delta_seg_body,
        grid=(b, ngrp, l // lb),
        in_specs=in_specs + [state_spec],
        out_specs=[y_spec, state_spec],
        out_shape=[jax.ShapeDtypeStruct((b, l, D_MODEL), BF16),
                   jax.ShapeDtypeStruct(s0.shape, F32)],
        compiler_params=_cparams("parallel", "parallel", "parallel"),
        name="delta_net_seg",
    )(*args, s0)


def _mla_qkv_body(cq_ref, ckv_ref, pair_ref, cos_ref, sin_ref, qg_ref, kvg_ref, wm_ref, ws_ref, wuk_ref, wuv_ref,
                  q_ref, ckv_out, kpe_out, kf_ref, v_ref, *, expand_kv):
    cos_t = cos_ref[...]
    sin_t = sin_ref[...]
    cqn = _rms_rows(cq_ref[...], qg_ref[...]).astype(BF16)
    ckv = _rms_rows(ckv_ref[...], kvg_ref[...])
    ckv_out[...] = ckv
    pair = pair_ref[...]
    kpe = pair[:, :128] * cos_t + pair[:, 128:] * sin_t
    kpe_out[...] = kpe
    ckv16 = ckv.astype(BF16)
    kpe16 = kpe.astype(BF16)
    for h in range(HEADS):
        qm = _dot(cqn, wm_ref[:, h * QK_PAD:(h + 1) * QK_PAD])
        qs = _dot(cqn, ws_ref[:, h * 128:(h + 1) * 128])
        q_ref[:, h * QK_PAD:h * QK_PAD + 128] = (qm[:, :128] * MLA_SCALE).astype(BF16)
        q_ref[:, h * QK_PAD + 128:(h + 1) * QK_PAD] = ((qm[:, 128:] * cos_t + qs * sin_t) * MLA_SCALE).astype(BF16)
        if expand_kv:
            kf_ref[:, h * QK_PAD:h * QK_PAD + 128] = _dot(ckv16, wuk_ref[:, h * 128:(h + 1) * 128]).astype(BF16)
            kf_ref[:, h * QK_PAD + 128:(h + 1) * QK_PAD] = kpe16
            v_ref[:, h * 128:(h + 1) * 128] = _dot(ckv16, wuv_ref[:, h * 128:(h + 1) * 128]).astype(BF16)
    if not expand_kv:
        kf_ref[...] = jnp.zeros(kf_ref.shape, kf_ref.dtype)
        v_ref[...] = jnp.zeros(v_ref.shape, v_ref.dtype)


def _mla_qkv(proj, cos_t, sin_t, qg, kvg, wm, ws, wuk, wuv, tm, expand_kv):
    t = proj.shape[0]
    kf_shape = (t, HEADS * QK_PAD) if expand_kv else (8, 128)
    v_shape = (t, HEADS * V_DIM) if expand_kv else (8, 128)
    kf_spec = (pl.BlockSpec((tm, HEADS * QK_PAD), lambda i: (i, 0)) if expand_kv
               else pl.BlockSpec((8, 128), lambda i: (0, 0)))
    v_spec = (pl.BlockSpec((tm, HEADS * V_DIM), lambda i: (i, 0)) if expand_kv
              else pl.BlockSpec((8, 128), lambda i: (0, 0)))
    const = lambda shape: pl.BlockSpec(shape, lambda i: (0, 0))
    return pl.pallas_call(
        functools.partial(_mla_qkv_body, expand_kv=expand_kv),
        grid=(t // tm,),
        in_specs=[pl.BlockSpec((tm, Q_LORA), lambda i: (i, 0)),
                  pl.BlockSpec((tm, KV_LORA), lambda i: (i, 1)),
                  pl.BlockSpec((tm, 256), lambda i: (i, PAIR_COL0 // 256)),
                  pl.BlockSpec((tm, 128), lambda i: (i, 0)),
                  pl.BlockSpec((tm, 128), lambda i: (i, 0)),
                  const((1, Q_LORA)), const((1, KV_LORA)),
                  const(wm.shape), const(ws.shape), const(wuk.shape), const(wuv.shape)],
        out_specs=[pl.BlockSpec((tm, HEADS * QK_PAD), lambda i: (i, 0)),
                   pl.BlockSpec((tm, KV_LORA), lambda i: (i, 0)),
                   pl.BlockSpec((tm, 128), lambda i: (i, 0)),
                   kf_spec, v_spec],
        out_shape=[jax.ShapeDtypeStruct((t, HEADS * QK_PAD), BF16),
                   jax.ShapeDtypeStruct((t, KV_LORA), F32),
                   jax.ShapeDtypeStruct((t, 128), F32),
                   jax.ShapeDtypeStruct(kf_shape, BF16),
                   jax.ShapeDtypeStruct(v_shape, BF16)],
        compiler_params=_cparams("arbitrary"),
        name="mla_qkv",
    )(proj, proj, proj, cos_t, sin_t, qg, kvg, wm, ws, wuk, wuv)


def _flash_body(q_ref, k_ref, v_ref, gate_ref, o_ref, *, tq, n_q, n_split):
    qi = pl.program_id(2)
    rows = tq // n_split

    def scores(q_parts, j, diagonal):
        if not diagonal:
            k = k_ref[0, j * tq:(j + 1) * tq, :]
            return [_dot_nt(qp, k) for qp in q_parts]
        return [_dot_nt(qp, k_ref[0, j * tq:j * tq + (i + 1) * rows, :]) for i, qp in enumerate(q_parts)]

    for v in range(n_q):
        @pl.when(qi == v)
        def _(v=v):
            q_parts = [q_ref[0, i * rows:(i + 1) * rows, :] for i in range(n_split)]
            m = [jnp.full((rows, 128), -jnp.inf, F32) for _ in range(n_split)]
            l = [jnp.zeros((rows, 128), F32) for _ in range(n_split)]
            acc = [jnp.zeros((rows, V_DIM), F32) for _ in range(n_split)]
            s_next = scores(q_parts, 0, v == 0)
            for j in range(v + 1):
                s_cur = s_next
                if j < v:
                    s_next = scores(q_parts, j + 1, j + 1 == v)
                for i in range(n_split):
                    s = s_cur[i]
                    n_keys = s.shape[1]
                    if j == v:
                        ri = lax.broadcasted_iota(jnp.int32, s.shape, 0) + i * rows
                        ci = lax.broadcasted_iota(jnp.int32, s.shape, 1)
                        s = jnp.where(ci <= ri, s, -jnp.inf)
                    m_new = jnp.maximum(m[i], jnp.max(s, axis=1, keepdims=True))
                    alpha = jnp.exp(m[i] - m_new)
                    e = jnp.exp(s - m_new[:, :1])
                    l[i] = alpha * l[i] + jnp.sum(e, axis=1, keepdims=True)
                    acc[i] = alpha * acc[i] + _dot(e.astype(BF16), v_ref[0, j * tq:j * tq + n_keys, :])
                    m[i] = m_new
            for i in range(n_split):
                rs = slice(i * rows, (i + 1) * rows)
                o_ref[0, rs, :] = (acc[i] / l[i] * _silu(gate_ref[0, rs, :])).astype(o_ref.dtype)


def _prompt_attention(q3, k3, v3, proj3, tq):
    b, l, _ = q3.shape
    return pl.pallas_call(
        functools.partial(_flash_body, tq=tq, n_q=l // tq, n_split=2),
        grid=(b, HEADS, l // tq),
        in_specs=[pl.BlockSpec((1, tq, QK_PAD), lambda bi, h, qi: (bi, qi, h)),
                  pl.BlockSpec((1, l, QK_PAD), lambda bi, h, qi: (bi, 0, h)),
                  pl.BlockSpec((1, l, V_DIM), lambda bi, h, qi: (bi, 0, h)),
                  pl.BlockSpec((1, tq, V_DIM), lambda bi, h, qi: (bi, qi, GATE_COL0 // V_DIM + h))],
        out_specs=pl.BlockSpec((1, tq, V_DIM), lambda bi, h, qi: (bi, qi, h)),
        out_shape=jax.ShapeDtypeStruct((b, l, HEADS * V_DIM), BF16),
        compiler_params=_cparams("parallel", "parallel", "arbitrary"),
        name="prompt_attention",
    )(q3, k3, v3, proj3)


def _head_mm_body(x_ref, w_ref, o_ref):
    o_ref[0] = _dot(x_ref[...], w_ref[0]).astype(o_ref.dtype)


def _absorb_queries(q, w_ukt):
    t = q.shape[0]
    return pl.pallas_call(
        _head_mm_body,
        grid=(HEADS,),
        in_specs=[pl.BlockSpec((t, NOPE), lambda h: (0, 2 * h)),
                  pl.BlockSpec((1, NOPE, KV_LORA), lambda h: (h, 0, 0))],
        out_specs=pl.BlockSpec((1, t, KV_LORA), lambda h: (h, 0, 0)),
        out_shape=jax.ShapeDtypeStruct((HEADS, t, KV_LORA), BF16),
        compiler_params=_cparams("parallel"),
        name="absorb_queries",
    )(q, w_ukt)


def _expand_values_body(x_ref, w_ref, gate_ref, o_ref):
    o_ref[...] = (_dot(x_ref[0], w_ref[0]) * _silu(gate_ref[...])).astype(o_ref.dtype)


def _expand_values(o_lat, w_uvh, proj):
    t = o_lat.shape[1]
    return pl.pallas_call(
        _expand_values_body,
        grid=(HEADS,),
        in_specs=[pl.BlockSpec((1, t, KV_LORA), lambda h: (h, 0, 0)),
                  pl.BlockSpec((1, KV_LORA, V_DIM), lambda h: (h, 0, 0)),
                  pl.BlockSpec((t, V_DIM), lambda h: (0, GATE_COL0 // V_DIM + h))],
        out_specs=pl.BlockSpec((t, V_DIM), lambda h: (0, h)),
        out_shape=jax.ShapeDtypeStruct((t, HEADS * V_DIM), BF16),
        compiler_params=_cparams("parallel"),
        name="expand_values",
    )(o_lat, w_uvh, proj)


def _paged_attn_body(pt_ref, ql_ref, qr_ref, cn_ref, rn_ref, *rest, dec_seq):
    npg = PAGES_PER_STEP
    c_refs, r_refs = rest[:npg], rest[npg:2 * npg]
    o_ref, m_ref, l_ref, acc_ref, cbuf_even, sbuf_even, cbuf_odd, sbuf_odd = rest[2 * npg:]
    j = pl.program_id(1)
    last = pl.num_programs(1) - 1
    ql = ql_ref[0]
    qr = qr_ref[0]

    def fold(s, values):
        m_prev = m_ref[...]
        m_new = jnp.maximum(m_prev, jnp.max(s, axis=1, keepdims=True))
        alpha = jnp.exp(m_prev - m_new)
        e = jnp.exp(s - m_new[:, :1])
        l_ref[...] = alpha * l_ref[...] + jnp.sum(e, axis=1, keepdims=True)
        acc_ref[...] = alpha[:, :1] * acc_ref[...] + _dot(e.astype(BF16), values)
        m_ref[...] = m_new

    @pl.when(j == 0)
    def _():
        m_ref[...] = jnp.full(m_ref.shape, -jnp.inf, F32)
        l_ref[...] = jnp.zeros(l_ref.shape, F32)
        acc_ref[...] = jnp.zeros(acc_ref.shape, F32)
        c = cn_ref[0].astype(BF16)
        s = _dot_nt(ql, c) + _dot_nt(qr, rn_ref[0].astype(BF16))
        ri = lax.broadcasted_iota(jnp.int32, s.shape, 0)
        ci = lax.broadcasted_iota(jnp.int32, s.shape, 1)
        fold(jnp.where(ci <= ri % dec_seq, s, -jnp.inf), c)
        sbuf_odd[...] = jnp.full(sbuf_odd.shape, -jnp.inf, F32)
        cbuf_odd[...] = jnp.zeros(cbuf_odd.shape, BF16)

    for parity, (cbuf, sbuf, cbuf_prev, sbuf_prev) in enumerate(((cbuf_even, sbuf_even, cbuf_odd, sbuf_odd),
                                                                 (cbuf_odd, sbuf_odd, cbuf_even, sbuf_even))):
        @pl.when(j % 2 == parity)
        def _(cbuf=cbuf, sbuf=sbuf, cbuf_prev=cbuf_prev, sbuf_prev=sbuf_prev):
            for i, (c_ref, r_ref) in enumerate(zip(c_refs, r_refs)):
                c = c_ref[0, 0].astype(BF16)
                cbuf[i * PAGE_SIZE:(i + 1) * PAGE_SIZE, :] = c
                sbuf[:, i * PAGE_SIZE:(i + 1) * PAGE_SIZE] = _dot_nt(ql, c) + _dot(qr, r_ref[0, 0].astype(BF16))
            fold(sbuf_prev[...], cbuf_prev[...])

            @pl.when(j == last)
            def _():
                fold(sbuf[...], cbuf[...])
                o_ref[0] = (acc_ref[...] / l_ref[...][:, :1]).astype(o_ref.dtype)


def _paged_attention(q_lat, q_r, ckv_new, kpe_new, cache_ckv, cache_kpe_t, page_table_flat, n_pages, dec_seq):
    b, rows, _ = q_lat.shape
    npg = PAGES_PER_STEP

    def page_spec(shape, i):
        return pl.BlockSpec((1, 1) + shape, lambda bi, j, pt, i=i: (0, pt[bi * n_pages + j * npg + i], 0, 0))

    grid_spec = pltpu.PrefetchScalarGridSpec(
        num_scalar_prefetch=1,
        grid=(b, n_pages // npg),
        in_specs=([pl.BlockSpec((1, rows, KV_LORA), lambda bi, j, pt: (bi, 0, 0)),
                   pl.BlockSpec((1, rows, ROPE), lambda bi, j, pt: (bi, 0, 0)),
                   pl.BlockSpec((1, PAGE_SIZE, KV_LORA), lambda bi, j, pt: (bi, 0, 0)),
                   pl.BlockSpec((1, PAGE_SIZE, ROPE), lambda bi, j, pt: (bi, 0, 0))]
                  + [page_spec((PAGE_SIZE, KV_LORA), i) for i in range(npg)]
                  + [page_spec((ROPE, PAGE_SIZE), i) for i in range(npg)]),
        out_specs=pl.BlockSpec((1, rows, KV_LORA), lambda bi, j, pt: (bi, 0, 0)),
        scratch_shapes=[pltpu.VMEM((rows, 128), F32), pltpu.VMEM((rows, 128), F32),
                        pltpu.VMEM((rows, KV_LORA), F32),
                        pltpu.VMEM((npg * PAGE_SIZE, KV_LORA), BF16),
                        pltpu.VMEM((rows, npg * PAGE_SIZE), F32),
                        pltpu.VMEM((npg * PAGE_SIZE, KV_LORA), BF16),
                        pltpu.VMEM((rows, npg * PAGE_SIZE), F32)],
    )
    return pl.pallas_call(
        functools.partial(_paged_attn_body, dec_seq=dec_seq),
        grid_spec=grid_spec,
        out_shape=jax.ShapeDtypeStruct((b, rows, KV_LORA), BF16),
        compiler_params=_cparams("parallel", "arbitrary"),
        name="paged_attention",
    )(page_table_flat, q_lat, q_r, ckv_new, kpe_new, *([cache_ckv] * npg), *([cache_kpe_t] * npg))


def _prepare_weights(norm_g, ple_norm_g, w_ple_gate, w_ple_proj, final_norm_g, w_in_even, w_out_even, conv_a_w,
                     conv_qkv_w, dn_a_log, dn_dt_bias, dn_norm_g, w_in_mla, mla_q_norm_g, mla_kv_norm_g, w_uq,
                     w_uk, w_uv, w_o_mla):
    d = D_MODEL
    w = {}
    w["norm_g"] = norm_g.reshape(2, 1, d)
    w["ple_norm_g"] = ple_norm_g.reshape(2, 1, d)
    w["final_norm_g"] = final_norm_g.reshape(1, d)
    w["w_ple_gate"] = w_ple_gate.astype(BF16)
    w["w_ple_proj"] = w_ple_proj.astype(BF16)
    w_in = w_in_even[0]
    w["w_in_main"] = w_in[:, :8 * d].astype(BF16)
    w["w_in_gates"] = jnp.pad(w_in[:, 8 * d:], ((0, 0), (0, 128 - 2 * HEADS))).astype(BF16)
    w["a_log_row"] = jnp.pad(dn_a_log[0], (HEADS, 128 - 2 * HEADS)).reshape(1, 128)
    w["dt_bias_row"] = jnp.pad(dn_dt_bias[0], (HEADS, 128 - 2 * HEADS)).reshape(1, 128)
    w["w_out_a"] = w_out_even[0, :d].astype(BF16)
    w["w_out_b"] = w_out_even[0, d:].astype(BF16)
    w["conv_a_w8"] = jnp.pad(conv_a_w[0], ((0, 8 - SC_CONV), (0, 0)))
    w["conv_qkv_w8"] = jnp.pad(conv_qkv_w[0], ((0, 8 - DN_CONV), (0, 0)))
    w["dn_norm_g"] = dn_norm_g[0].reshape(1, HEAD_DIM)
    wi = w_in_mla[0]
    k_pe = wi[:, 2 * Q_LORA:2 * Q_LORA + ROPE]
    half = ROPE // 2
    zeros64 = jnp.zeros((d, 128 - ROPE), F32)
    w["w_in_mla"] = jnp.concatenate(
        [wi[:, :2 * Q_LORA], wi[:, 2 * Q_LORA + ROPE:], k_pe, zeros64, k_pe[:, half:], k_pe[:, :half], zeros64,
         jnp.zeros((d, MLA_IN_PAD - PAIR_COL0 - 256), F32)], axis=1).astype(BF16)
    w["mla_q_norm_g"] = mla_q_norm_g[0].reshape(1, Q_LORA)
    w["mla_kv_norm_g"] = mla_kv_norm_g[0].reshape(1, KV_LORA)
    uq = w_uq[0].reshape(Q_LORA, HEADS, NOPE + ROPE)
    x1, x2 = uq[:, :, NOPE:NOPE + half], uq[:, :, NOPE + half:]
    zpad = jnp.zeros((Q_LORA, HEADS, QK_PAD - NOPE - ROPE), F32)
    w["w_uq_main"] = jnp.concatenate([uq, zpad], axis=2).reshape(Q_LORA, HEADS * QK_PAD).astype(BF16)
    w["w_uq_swap"] = jnp.concatenate([x2, x1, zpad], axis=2).reshape(Q_LORA, HEADS * 128).astype(BF16)
    w["w_uk"] = w_uk[0].reshape(KV_LORA, HEADS * NOPE).astype(BF16)
    w["w_uv"] = w_uv[0].reshape(KV_LORA, HEADS * V_DIM).astype(BF16)
    w["w_uk_t"] = jnp.transpose(w_uk[0], (1, 2, 0)).astype(BF16)
    w["w_uv_h"] = jnp.transpose(w_uv[0], (1, 0, 2)).astype(BF16)
    w["w_o_mla"] = w_o_mla[0].astype(BF16)
    return w


def _rope_tables(pos):
    half = ROPE // 2
    inv = ROPE_THETA ** (-jnp.arange(half, dtype=F32) / half)
    ang = pos.astype(F32)[:, None] * inv
    cos, sin = jnp.cos(ang), jnp.sin(ang)
    z = jnp.zeros((pos.shape[0], 128 - ROPE), F32)
    return jnp.concatenate([cos, cos, z], axis=1), jnp.concatenate([-sin, sin, z], axis=1)


def _group_gates(gates3):
    b, l, _ = gates3.shape
    ngrp = HEADS // HEAD_GROUP
    beta = gates3[:, :, :HEADS].reshape(b, l, ngrp, HEAD_GROUP)
    g = gates3[:, :, HEADS:2 * HEADS].reshape(b, l, ngrp, HEAD_GROUP)
    grouped = jnp.transpose(jnp.concatenate([beta, g], axis=-1), (0, 2, 1, 3))
    return jnp.pad(grouped, ((0, 0), (0, 0), (0, 0), (0, 128 - 2 * HEAD_GROUP)))


def _layer_stack(x, p, pos, conv_a0, conv_qkv0, delta0, past, w, tm):
    b, l, d = x.shape
    t = b * l
    h = x.reshape(t, d)

    proj = _rms_matmul(h, w["norm_g"][0], w["w_in_main"], F32, tm, 1024, "in_proj_even")
    gates = _rms_matmul_gates(h, w["norm_g"][0], w["w_in_gates"], w["a_log_row"], w["dt_bias_row"], tm)
    proj3 = proj.reshape(b, l, 8 * d)
    gates3 = gates.reshape(b, l, 128)
    conv_qkv1 = proj3[:, l - (DN_CONV - 1):l, 4 * d:7 * d]
    if past is None:
        y_a, a_last8 = _mixer_a(proj3, w["conv_a_w8"], 512, 512, False)
        y_b, delta1 = _delta_net(proj3, _group_gates(gates3), w["conv_qkv_w8"], None, w["dn_norm_g"], 256)
        conv_a1 = a_last8[:, 8 - (SC_CONV - 1):]
    else:
        n_state = DN_CONV - 1
        tail = SEG_ROWS - n_state - l
        a_state = jnp.pad(conv_a0, ((0, 0), (n_state - (SC_CONV - 1), 0), (0, 0)))
        a_ones = jnp.pad(jnp.ones((b, SC_CONV - 1, d), F32), ((0, 0), (n_state - (SC_CONV - 1), 0), (0, 0)))
        state_rows = jnp.concatenate([a_state, a_ones, jnp.zeros((b, n_state, 2 * d), F32), conv_qkv0,
                                      jnp.zeros((b, n_state, d), F32)], axis=2)
        stacked = jnp.concatenate([state_rows, proj3, jnp.zeros((b, tail, 8 * d), F32)], axis=1)
        stacked = stacked.reshape(1, b * SEG_ROWS, 8 * d)
        gates_s = jnp.pad(gates3, ((0, 0), (n_state, tail), (0, 0))).reshape(1, b * SEG_ROWS, 128)
        tl = min(b * SEG_ROWS, 512)
        y_a, prod = _mixer_a(stacked, w["conv_a_w8"], tl, 512, True)
        y_b, delta1 = _delta_net(stacked, _group_gates(gates_s), w["conv_qkv_w8"], delta0, w["dn_norm_g"], DN_CHUNK)
        tok = slice(n_state, n_state + l)
        y_a = y_a.reshape(b, SEG_ROWS, d)[:, tok]
        y_b = y_b.reshape(b, SEG_ROWS, d)[:, tok]
        conv_a1 = prod.reshape(b, SEG_ROWS, d)[:, n_state + l - (SC_CONV - 1):n_state + l]
    h = _matmul_residual([y_a.reshape(t, d), y_b.reshape(t, d)], [w["w_out_a"], w["w_out_b"]], h, min(tm, 512),
                         1024, "out_proj_even")
    h = _ple(h, w["ple_norm_g"][0], w["w_ple_gate"][0], p[0].reshape(t, PLE_DIM), w["w_ple_proj"][0],
             w["final_norm_g"], False, min(tm, 512), "ple0")

    proj = _rms_matmul(h, w["norm_g"][1], w["w_in_mla"], F32, tm, 512, "in_proj_mla")
    cos_t, sin_t = _rope_tables(pos)
    cos_t = jnp.broadcast_to(cos_t[None], (b, l, 128)).reshape(t, 128)
    sin_t = jnp.broadcast_to(sin_t[None], (b, l, 128)).reshape(t, 128)
    q, ckv, kpe128, kfull, v = _mla_qkv(proj, cos_t, sin_t, w["mla_q_norm_g"], w["mla_kv_norm_g"], w["w_uq_main"],
                                        w["w_uq_swap"], w["w_uk"], w["w_uv"], min(t, 256), past is None)
    kpe = kpe128[:, :ROPE]
    if past is None:
        o = _prompt_attention(q.reshape(b, l, HEADS * QK_PAD), kfull.reshape(b, l, HEADS * QK_PAD),
                              v.reshape(b, l, HEADS * V_DIM), proj.reshape(b, l, MLA_IN_PAD), 512)
        o = o.reshape(t, HEADS * V_DIM)
    else:
        cache_ckv, cache_kpe, page_table = past
        n_pages = page_table.shape[1]
        q_lat = _absorb_queries(q, w["w_uk_t"])
        q_lat = jnp.transpose(q_lat.reshape(HEADS, b, l, KV_LORA), (1, 0, 2, 3)).reshape(b, HEADS * l, KV_LORA)
        q_r = q.reshape(b, l, HEADS, QK_PAD)[:, :, :, NOPE:NOPE + ROPE]
        q_r = jnp.transpose(q_r, (0, 2, 1, 3)).reshape(b, HEADS * l, ROPE)
        ckv_new = jnp.pad(ckv.reshape(b, l, KV_LORA), ((0, 0), (0, PAGE_SIZE - l), (0, 0)))
        kpe_new = jnp.pad(kpe.reshape(b, l, ROPE), ((0, 0), (0, PAGE_SIZE - l), (0, 0)))
        o_lat = _paged_attention(q_lat, q_r, ckv_new, kpe_new, cache_ckv, jnp.swapaxes(cache_kpe, 2, 3),
                                 page_table.reshape(-1), n_pages, l)
        o_lat = jnp.transpose(o_lat.reshape(b, HEADS, l, KV_LORA), (1, 0, 2, 3)).reshape(HEADS, t, KV_LORA)
        o = _expand_values(o_lat, w["w_uv_h"], proj)
    h = _matmul_residual([o], [w["w_o_mla"]], h, min(tm, 512), 1024, "out_proj_mla")
    y = _ple(h, w["ple_norm_g"][1], w["w_ple_gate"][1], p[1].reshape(t, PLE_DIM), w["w_ple_proj"][1],
             w["final_norm_g"], True, min(tm, 512), "ple1")
    return (y.reshape(b, l, d), conv_a1[None], conv_qkv1[None], delta1[None],
            ckv.reshape(1, b, l, KV_LORA), kpe.reshape(1, b, l, ROPE))


def kernel(x_prompt, x_sample, cache_ckv, cache_kpe, state_conv_a, state_conv_qkv, state_delta, page_table,
           p_prompt, p_sample, norm_g, ple_norm_g, w_ple_gate, w_ple_proj, final_norm_g, w_in_even, w_out_even,
           conv_a_w, conv_qkv_w, dn_a_log, dn_dt_bias, dn_norm_g, w_in_mla, mla_q_norm_g, mla_kv_norm_g, w_uq,
           w_uk, w_uv, w_o_mla):
    w = _prepare_weights(norm_g, ple_norm_g, w_ple_gate, w_ple_proj, final_norm_g, w_in_even, w_out_even, conv_a_w,
                         conv_qkv_w, dn_a_log, dn_dt_bias, dn_norm_g, w_in_mla, mla_q_norm_g, mla_kv_norm_g, w_uq,
                         w_uk, w_uv, w_o_mla)
    seq = x_prompt.shape[1]
    dec_b, dec_l, _ = x_sample.shape
    past_len = page_table.shape[1] * PAGE_SIZE
    outs_p = _layer_stack(x_prompt, p_prompt, jnp.arange(seq), None, None, None, None, w, 1024)
    outs_s = _layer_stack(x_sample, p_sample, past_len + jnp.arange(dec_l), state_conv_a[0], state_conv_qkv[0],
                          state_delta[0], (cache_ckv, cache_kpe, page_table), w, dec_b * dec_l)
    return (outs_p[0], outs_s[0]) + tuple(outs_p[1:]) + tuple(outs_s[1:])
```

```python
import functools
import math

import jax
import jax.numpy as jnp
from jax import lax
from jax.experimental import pallas as pl
from jax.experimental.pallas import tpu as pltpu

F32 = jnp.float32
BF16 = jnp.bfloat16

NORM_EPS = 1e-6
D_MODEL = 2048
HEADS = 16
HEAD_DIM = 128
DN_CHUNK = 64
DN_CONV = 4
SC_CONV = 3
Q_LORA = 512
KV_LORA = 512
NOPE = 128
ROPE = 64
V_DIM = 128
PLE_DIM = 256
ROPE_THETA = 10000.0
PAGE_SIZE = 128
MLA_SCALE = (NOPE + ROPE) ** -0.5
QK_PAD = 256
MLA_IN_PAD = 3584
GATE_COL0 = Q_LORA + KV_LORA
PAIR_COL0 = GATE_COL0 + HEADS * V_DIM
HEAD_GROUP = 8
SEG_ROWS = 8
PAGES_PER_STEP = 32

V7X_VMEM_LIMIT_BYTES = 56 * 1024 * 1024


def _cparams(*sem):
    return pltpu.CompilerParams(dimension_semantics=sem, vmem_limit_bytes=V7X_VMEM_LIMIT_BYTES)


def _sigmoid(x):
    return 1.0 / (1.0 + jnp.exp(-x))


def _silu(x):
    return x * _sigmoid(x)


def _rms_rows(x, g):
    return x * lax.rsqrt(jnp.mean(x * x, axis=-1, keepdims=True) + NORM_EPS) * g


def _dot(a, b):
    return jnp.dot(a, b, preferred_element_type=F32)


def _dot_nt(a, b):
    return lax.dot_general(a, b, (((1,), (1,)), ((), ())), preferred_element_type=F32)


def _rms_mm_body(x_ref, g_ref, w_ref, o_ref, xn_ref):
    @pl.when(pl.program_id(1) == 0)
    def _():
        xn_ref[...] = _rms_rows(x_ref[...], g_ref[...]).astype(BF16)

    o_ref[...] = _dot(xn_ref[...], w_ref[...]).astype(o_ref.dtype)


def _rms_matmul(x, g, w, out_dtype, tm, tn, name):
    t, k = x.shape
    n = w.shape[1]
    return pl.pallas_call(
        _rms_mm_body,
        grid=(t // tm, n // tn),
        in_specs=[pl.BlockSpec((tm, k), lambda i, j: (i, 0)),
                  pl.BlockSpec((1, k), lambda i, j: (0, 0)),
                  pl.BlockSpec((k, tn), lambda i, j: (0, j))],
        out_specs=pl.BlockSpec((tm, tn), lambda i, j: (i, j)),
        out_shape=jax.ShapeDtypeStruct((t, n), out_dtype),
        scratch_shapes=[pltpu.VMEM((tm, k), BF16)],
        compiler_params=_cparams("parallel", "arbitrary"),
        name=name,
    )(x, g, w)


def _rms_mm_gates_body(x_ref, g_ref, w_ref, alog_ref, dtb_ref, o_ref):
    xn = _rms_rows(x_ref[...], g_ref[...]).astype(BF16)
    y = _dot(xn, w_ref[...])
    lane = lax.broadcasted_iota(jnp.int32, y.shape, 1)
    beta = _sigmoid(y)
    t = y + dtb_ref[...]
    softplus = jnp.maximum(t, 0.0) + jnp.log1p(jnp.exp(-jnp.abs(t)))
    decay = -jnp.exp(alog_ref[...]) * softplus
    o_ref[...] = jnp.where(lane < HEADS, beta, decay)


def _rms_matmul_gates(x, g, w, alog_row, dtb_row, tm):
    t, k = x.shape
    return pl.pallas_call(
        _rms_mm_gates_body,
        grid=(t // tm,),
        in_specs=[pl.BlockSpec((tm, k), lambda i: (i, 0)),
                  pl.BlockSpec((1, k), lambda i: (0, 0)),
                  pl.BlockSpec((k, 128), lambda i: (0, 0)),
                  pl.BlockSpec((1, 128), lambda i: (0, 0)),
                  pl.BlockSpec((1, 128), lambda i: (0, 0))],
        out_specs=pl.BlockSpec((tm, 128), lambda i: (i, 0)),
        out_shape=jax.ShapeDtypeStruct((t, 128), F32),
        compiler_params=_cparams("parallel"),
        name="dn_gates",
    )(x, g, w, alog_row, dtb_row)


def _mm_res_body(*refs, n_in):
    xs, ws, res_ref, o_ref = refs[:n_in], refs[n_in:2 * n_in], refs[2 * n_in], refs[2 * n_in + 1]
    acc = res_ref[...]
    for x_ref, w_ref in zip(xs, ws):
        acc = acc + _dot(x_ref[...], w_ref[...])
    o_ref[...] = acc


def _matmul_residual(xs, ws, res, tm, tn, name):
    t, n = res.shape
    n_in = len(xs)
    in_specs = ([pl.BlockSpec((tm, x.shape[1]), lambda i, j: (i, 0)) for x in xs]
                + [pl.BlockSpec((w.shape[0], tn), lambda i, j: (0, j)) for w in ws]
                + [pl.BlockSpec((tm, tn), lambda i, j: (i, j))])
    return pl.pallas_call(
        functools.partial(_mm_res_body, n_in=n_in),
        grid=(t // tm, n // tn),
        in_specs=in_specs,
        out_specs=pl.BlockSpec((tm, tn), lambda i, j: (i, j)),
        out_shape=jax.ShapeDtypeStruct((t, n), F32),
        compiler_params=_cparams("parallel", "arbitrary"),
        name=name,
    )(*xs, *ws, res)


def _ple_body(h_ref, g_ref, wg_ref, p_ref, wp_ref, fg_ref, o_ref, *, final_norm):
    h = h_ref[...]
    xn = _rms_rows(h, g_ref[...]).astype(BF16)
    gate = _sigmoid(_dot(xn, wg_ref[...]))
    h2 = h + gate * _dot(p_ref[...].astype(BF16), wp_ref[...])
    if final_norm:
        h2 = _rms_rows(h2, fg_ref[...])
    o_ref[...] = h2


def _ple(h, g, w_gate, p_all, layer, w_proj, final_g, final_norm, tm, name):
    t, d = h.shape
    pd = p_all.shape[1]
    p_block0 = layer * (t // tm)
    return pl.pallas_call(
        functools.partial(_ple_body, final_norm=final_norm),
        grid=(t // tm,),
        in_specs=[pl.BlockSpec((tm, d), lambda i: (i, 0)),
                  pl.BlockSpec((1, d), lambda i: (0, 0)),
                  pl.BlockSpec((d, d), lambda i: (0, 0)),
                  pl.BlockSpec((tm, pd), lambda i: (p_block0 + i, 0)),
                  pl.BlockSpec((pd, d), lambda i: (0, 0)),
                  pl.BlockSpec((1, d), lambda i: (0, 0))],
        out_specs=pl.BlockSpec((tm, d), lambda i: (i, 0)),
        out_shape=jax.ShapeDtypeStruct((t, d), F32),
        compiler_params=_cparams("parallel"),
        name=name,
    )(h, g, w_gate, p_all, w_proj, final_g)


def _mixer_a_body(ac_ref, ah_ref, ab_ref, az_ref, w_ref, y_ref, st_ref, carry_ref, *, emit_prod):
    @pl.when(pl.program_id(2) == 0)
    def _():
        carry_ref[...] = jnp.zeros(carry_ref.shape, F32)

    prod = ac_ref[0].astype(F32) * ah_ref[0].astype(F32)
    tl = prod.shape[0]
    row = lax.broadcasted_iota(jnp.int32, prod.shape, 0)
    c6 = carry_ref[6:7, :]
    c7 = carry_ref[7:8, :]
    p1 = jnp.where(row == 0, c7, pltpu.roll(prod, 1, axis=0))
    p2 = jnp.where(row == 0, c6, jnp.where(row == 1, c7, pltpu.roll(prod, 2, axis=0)))
    w = w_ref[...]
    u = w[2:3, :] * prod + w[1:2, :] * p1 + w[0:1, :] * p2
    y_ref[0] = (ab_ref[0].astype(F32) * u * _silu(az_ref[0].astype(F32))).astype(y_ref.dtype)
    last = prod[tl - 8:, :]
    carry_ref[...] = last
    st_ref[0] = prod if emit_prod else last


def _mixer_a(proj3, conv_w8, tl, tc, emit_prod):
    b, l, _ = proj3.shape
    nct = D_MODEL // tc
    st_rows = tl if emit_prod else 8

    def col(gi):
        return pl.BlockSpec((1, tl, tc), lambda bi, ci, li, gi=gi: (bi, li, gi * nct + ci))

    return pl.pallas_call(
        functools.partial(_mixer_a_body, emit_prod=emit_prod),
        grid=(b, nct, l // tl),
        in_specs=[col(0), col(1), col(2), col(3),
                  pl.BlockSpec((8, tc), lambda bi, ci, li: (0, ci))],
        out_specs=[pl.BlockSpec((1, tl, tc), lambda bi, ci, li: (bi, li, ci)),
                   pl.BlockSpec((1, st_rows, tc), lambda bi, ci, li: (bi, li if emit_prod else 0, ci))],
        out_shape=[jax.ShapeDtypeStruct((b, l, D_MODEL), BF16),
                   jax.ShapeDtypeStruct((b, l if emit_prod else 8, D_MODEL), F32)],
        scratch_shapes=[pltpu.VMEM((8, tc), F32)],
        compiler_params=_cparams("parallel", "parallel", "arbitrary"),
        name="mixer_a",
    )(proj3, proj3, proj3, proj3, conv_w8)


def _segment_scan(x, seg):
    n = x.shape[0]
    row = lax.broadcasted_iota(jnp.int32, x.shape, 0)
    pos = row % seg
    cs = x
    s = 1
    while s < seg:
        cs = cs + jnp.where(pos >= s, pltpu.roll(cs, s, axis=0), 0.0)
        s *= 2
    if seg == n:
        return cs, jnp.broadcast_to(cs[n - 1:n, :], x.shape)
    tot = cs
    s = 1
    while s < seg:
        tot = jnp.where(pos + s < seg, pltpu.roll(tot, n - s, axis=0), tot)
        s *= 2
    return cs, tot


def _delta_prepare(q_ref, k_ref, v_ref, gb_ref, wq_ref, wk_ref, wv_ref, r0s, prev_rows, seg):
    c = DN_CHUNK
    items = [(n, r0, j) for n, r0 in enumerate(r0s) for j in range(HEAD_GROUP)]
    ri = lax.broadcasted_iota(jnp.int32, (c, c), 0)
    ci = lax.broadcasted_iota(jnp.int32, (c, c), 1)
    bcols, gcols, glasts = [], [], []
    for r0 in r0s:
        gates = gb_ref[0, 0, pl.ds(r0, c), :]
        gsum, gtot = _segment_scan(gates, seg)
        for j in range(HEAD_GROUP):
            bcols.append(jnp.broadcast_to(gates[:, j:j + 1], (c, HEAD_DIM)))
            gcols.append(jnp.broadcast_to(gsum[:, HEAD_GROUP + j:HEAD_GROUP + j + 1], (c, HEAD_DIM)))
            glasts.append(jnp.broadcast_to(gtot[:, HEAD_GROUP + j:HEAD_GROUP + j + 1], (c, HEAD_DIM)))

    def conv_silu(which, x_ref, w_ref, n, r0, j):
        lanes = slice(j * HEAD_DIM, (j + 1) * HEAD_DIM)
        cur = x_ref[0, pl.ds(r0, c), lanes].astype(F32)
        ext = jnp.concatenate([prev_rows(which, x_ref, lanes, n), cur], axis=0)
        w = w_ref[:, lanes]
        y = (w[3:4, :] * cur + w[2:3, :] * ext[7:7 + c, :] + w[1:2, :] * ext[6:6 + c, :]
             + w[0:1, :] * ext[5:5 + c, :])
        return _silu(y)

    qs = [conv_silu(0, q_ref, wq_ref, *item) for item in items]
    ks = [conv_silu(1, k_ref, wk_ref, *item) for item in items]
    vs = [conv_silu(2, v_ref, wv_ref, *item) for item in items]
    qs = [q * lax.rsqrt(jnp.sum(q * q, axis=-1, keepdims=True) + NORM_EPS) * (HEAD_DIM ** -0.5) for q in qs]
    ks = [k * lax.rsqrt(jnp.sum(k * k, axis=-1, keepdims=True) + NORM_EPS) for k in ks]
    kbs = [k * b for k, b in zip(ks, bcols)]
    m1s = [_dot_nt(jnp.concatenate([kb, q], axis=0).astype(BF16), k.astype(BF16))
           for kb, q, k in zip(kbs, qs, ks)]
    decays = []
    for gcol in gcols:
        grow = gcol.T[:c, :]
        decay = jnp.where(ri >= ci, jnp.exp(jnp.minimum(gcol[:, :c] - grow, 0.0)), 0.0)
        if seg < c:
            decay = jnp.where(ri // seg == ci // seg, decay, 0.0)
        decays.append(decay)
    powers = [jnp.where(ri > ci, m1[:c, :] * d, 0.0) for m1, d in zip(m1s, decays)]
    a_qks = [m1[c:, :] * d for m1, d in zip(m1s, decays)]
    t_invs = [jnp.where(ri == ci, 1.0, 0.0) - a for a in powers]
    span = 2
    while span < seg:
        p16s = [p.astype(BF16) for p in powers]
        powers = [_dot(p16, p16) for p16 in p16s]
        t_invs = [t + _dot(t.astype(BF16), p.astype(BF16)) for t, p in zip(t_invs, powers)]
        span *= 2
    egs = [jnp.exp(g) for g in gcols]
    sols = [_dot(t.astype(BF16), jnp.concatenate([v * b, kb * eg], axis=1).astype(BF16))
            for t, v, b, kb, eg in zip(t_invs, vs, bcols, kbs, egs)]
    k_dec_ts = [(k * jnp.exp(gl - g)).T for k, gl, g in zip(ks, glasts, gcols)]
    flat = [(sol[:, :HEAD_DIM], sol[:, HEAD_DIM:], q * eg, a_qk, k_dec_t, gl)
            for sol, q, eg, a_qk, k_dec_t, gl in zip(sols, qs, egs, a_qks, k_dec_ts, glasts)]
    return [flat[n * HEAD_GROUP:(n + 1) * HEAD_GROUP] for n in range(len(r0s))]


def _delta_seq_body(q_ref, k_ref, v_ref, z_ref, gb_ref, wq_ref, wk_ref, wv_ref, ng_ref, y_ref, s_ref,
                    s_scr, carry, u_scr, wq_scr, ak_scr, eg_scr):
    c = DN_CHUNK
    lb = q_ref.shape[1]
    n_chunks = lb // c

    @pl.when(pl.program_id(2) == 0)
    def _():
        s_scr[...] = jnp.zeros(s_scr.shape, F32)
        carry[...] = jnp.zeros(carry.shape, F32)

    heads = range(HEAD_GROUP)

    group = 2 if n_chunks % 2 == 0 else 1

    def prepare(gi, carry_val):
        r0s = [pl.multiple_of((gi * group + n) * c, c) for n in range(group)]

        def prev_rows(which, x_ref, lanes, n):
            if n > 0:
                return x_ref[0, pl.ds(pl.multiple_of(r0s[n] - 16, 16), 16), lanes].astype(F32)[8:, :]
            p0 = pl.multiple_of(jnp.maximum(r0s[0] - 16, 0), 16)
            return jnp.where(gi == 0, carry[which, :, lanes], x_ref[0, pl.ds(p0, 16), lanes].astype(F32)[8:, :])

        parts = _delta_prepare(q_ref, k_ref, v_ref, gb_ref, wq_ref, wk_ref, wv_ref, r0s, prev_rows, c)
        for n in range(group):
            ci = gi * group + n
            for j, (u, w, qe, a_qk, k_dec_t, glast) in enumerate(parts[n]):
                u_scr[ci, j] = u
                wq_scr[ci, j] = jnp.concatenate([w, qe], axis=0).astype(BF16)
                ak_scr[ci, j] = jnp.concatenate([a_qk, k_dec_t], axis=0).astype(BF16)
                eg_scr[ci, j] = jnp.exp(glast[:8, :])
        return carry_val

    lax.fori_loop(0, n_chunks // group, prepare, 0)

    def apply(ci, carry_val):
        r0 = pl.multiple_of(ci * c, c)
        s_prevs = [s_scr[j] for j in heads]
        r1s = [_dot(wq_scr[ci, j], s_prevs[j].astype(BF16)) for j in heads]
        v_news = [u_scr[ci, j] - r1s[j][:c, :] for j in heads]
        r2s = [_dot(ak_scr[ci, j], v_news[j].astype(BF16)) for j in heads]
        for j in heads:
            lanes = slice(j * HEAD_DIM, (j + 1) * HEAD_DIM)
            s_scr[j] = s_prevs[j] * eg_scr[ci, j][0:1, :] + r2s[j][c:, :]
            o = _rms_rows(r1s[j][c:, :] + r2s[j][:c, :], ng_ref[...]) * _silu(z_ref[0, pl.ds(r0, c), lanes].astype(F32))
            y_ref[0, pl.ds(r0, c), lanes] = o.astype(y_ref.dtype)
        return carry_val

    lax.fori_loop(0, n_chunks, apply, 0)
    carry[0] = q_ref[0, lb - 16:, :].astype(F32)[8:, :]
    carry[1] = k_ref[0, lb - 16:, :].astype(F32)[8:, :]
    carry[2] = v_ref[0, lb - 16:, :].astype(F32)[8:, :]
    s_ref[0] = s_scr[...]


def _delta_seg_body(q_ref, k_ref, v_ref, z_ref, gb_ref, wq_ref, wk_ref, wv_ref, ng_ref, s0_ref, y_ref, s_ref):
    c = DN_CHUNK
    seg = SEG_ROWS
    nseg = c // seg
    lb = q_ref.shape[1]
    zeros8 = jnp.zeros((8, HEAD_DIM), F32)
    lane_seg = lax.broadcasted_iota(jnp.int32, (HEAD_DIM, c), 1) // seg
    heads = range(HEAD_GROUP)
    for ci in range(lb // c):
        r0 = ci * c
        parts = _delta_prepare(q_ref, k_ref, v_ref, gb_ref, wq_ref, wk_ref, wv_ref, [r0],
                               lambda which, x_ref, lanes, n: zeros8, seg)[0]
        r1s = []
        for j, (u, w, qe, a_qk, k_dec_t, glast) in enumerate(parts):
            for p in range(nseg):
                rows = slice(p * seg, (p + 1) * seg)
                wq_p = jnp.concatenate([w[rows, :], qe[rows, :]], axis=0).astype(BF16)
                r1s.append(_dot(wq_p, s0_ref[ci * nseg + p, j].astype(BF16)))
        vn16s, o_firsts = [], []
        for j, (u, w, qe, a_qk, k_dec_t, glast) in enumerate(parts):
            mine = r1s[j * nseg:(j + 1) * nseg]
            v_new = u - jnp.concatenate([r1[:seg, :] for r1 in mine], axis=0)
            vn16s.append(v_new.astype(BF16))
            o_firsts.append(jnp.concatenate([r1[seg:, :] for r1 in mine], axis=0))
        o_seconds = [_dot(part[3].astype(BF16), vn16) for part, vn16 in zip(parts, vn16s)]
        for j, (u, w, qe, a_qk, k_dec_t, glast) in enumerate(parts):
            lanes = slice(j * HEAD_DIM, (j + 1) * HEAD_DIM)
            eg_last = jnp.exp(glast)
            for p in range(nseg):
                upd = _dot(jnp.where(lane_seg == p, k_dec_t, 0.0).astype(BF16), vn16s[j])
                s_ref[ci * nseg + p, j] = s0_ref[ci * nseg + p, j] * eg_last[p * seg:p * seg + 1, :] + upd
            o = _rms_rows(o_firsts[j] + o_seconds[j], ng_ref[...]) * _silu(z_ref[0, pl.ds(r0, c), lanes].astype(F32))
            y_ref[0, pl.ds(r0, c), lanes] = o.astype(y_ref.dtype)


def _delta_net(proj3, gates_g, conv_w8, s0, norm_g, lb):
    b, l, _ = proj3.shape
    gw = HEAD_GROUP * HEAD_DIM
    ngrp = HEADS // HEAD_GROUP
    per_d = D_MODEL // gw
    n_chunks = lb // DN_CHUNK

    def col(gi):
        return pl.BlockSpec((1, lb, gw), lambda bi, hi, li, gi=gi: (bi, li, gi * per_d + hi))

    def wcol(gi):
        return pl.BlockSpec((8, gw), lambda bi, hi, li, gi=gi: (0, gi * per_d + hi))

    in_specs = [col(4), col(5), col(6), col(7),
                pl.BlockSpec((1, 1, lb, 128), lambda bi, hi, li: (bi, hi, li, 0)),
                wcol(0), wcol(1), wcol(2),
                pl.BlockSpec((1, HEAD_DIM), lambda bi, hi, li: (0, 0))]
    y_spec = pl.BlockSpec((1, lb, gw), lambda bi, hi, li: (bi, li, hi))
    args = [proj3, proj3, proj3, proj3, gates_g, conv_w8, conv_w8, conv_w8, norm_g]
    if s0 is None:
        state_spec = pl.BlockSpec((1, HEAD_GROUP, HEAD_DIM, HEAD_DIM), lambda bi, hi, li: (bi, hi, 0, 0))
        return pl.pallas_call(
            _delta_seq_body,
            grid=(b, ngrp, l // lb),
            in_specs=in_specs,
            out_specs=[y_spec, state_spec],
            out_shape=[jax.ShapeDtypeStruct((b, l, D_MODEL), BF16),
                       jax.ShapeDtypeStruct((b, HEADS, HEAD_DIM, HEAD_DIM), F32)],
            scratch_shapes=[pltpu.VMEM((HEAD_GROUP, HEAD_DIM, HEAD_DIM), F32),
                            pltpu.VMEM((3, 8, gw), F32),
                            pltpu.VMEM((n_chunks, HEAD_GROUP, DN_CHUNK, HEAD_DIM), F32),
                            pltpu.VMEM((n_chunks, HEAD_GROUP, 2 * DN_CHUNK, HEAD_DIM), BF16),
                            pltpu.VMEM((n_chunks, HEAD_GROUP, DN_CHUNK + HEAD_DIM, DN_CHUNK), BF16),
                            pltpu.VMEM((n_chunks, HEAD_GROUP, 8, HEAD_DIM), F32)],
            compiler_params=_cparams("parallel", "parallel", "arbitrary"),
            name="delta_net_seq",
        )(*args)
    nseq = lb // SEG_ROWS
    state_spec = pl.BlockSpec((nseq, HEAD_GROUP, HEAD_DIM, HEAD_DIM), lambda bi, hi, li: (li, hi, 0, 0))
    return pl.pallas_call(
        _delta_seg_body,
        grid=(b, ngrp, l // lb),
        in_specs=in_specs + [state_spec],
        out_specs=[y_spec, state_spec],
        out_shape=[jax.ShapeDtypeStruct((b, l, D_MODEL), BF16),
                   jax.ShapeDtypeStruct(s0.shape, F32)],
        compiler_params=_cparams("parallel", "parallel", "parallel"),
        name="delta_net_seg",
    )(*args, s0)


def _mla_qkv_body(cq_ref, ckv_ref, pair_ref, cos_ref, sin_ref, qg_ref, kvg_ref, wm_ref, ws_ref, wuk_ref, wuv_ref,
                  q_ref, ckv_out, kpe_out, kf_ref, v_ref, *, expand_kv):
    cos_t = cos_ref[...]
    sin_t = sin_ref[...]
    cqn = _rms_rows(cq_ref[...], qg_ref[...]).astype(BF16)
    ckv = _rms_rows(ckv_ref[...], kvg_ref[...])
    ckv_out[...] = ckv
    pair = pair_ref[...]
    kpe = pair[:, :128] * cos_t + pair[:, 128:] * sin_t
    kpe_out[...] = kpe
    ckv16 = ckv.astype(BF16)
    kpe16 = kpe.astype(BF16)
    for h in range(HEADS):
        qm = _dot(cqn, wm_ref[:, h * QK_PAD:(h + 1) * QK_PAD])
        qs = _dot(cqn, ws_ref[:, h * 128:(h + 1) * 128])
        q_ref[:, h * QK_PAD:h * QK_PAD + 128] = (qm[:, :128] * MLA_SCALE).astype(BF16)
        q_ref[:, h * QK_PAD + 128:(h + 1) * QK_PAD] = ((qm[:, 128:] * cos_t + qs * sin_t) * MLA_SCALE).astype(BF16)
        if expand_kv:
            kf_ref[:, h * QK_PAD:h * QK_PAD + 128] = _dot(ckv16, wuk_ref[:, h * 128:(h + 1) * 128]).astype(BF16)
            kf_ref[:, h * QK_PAD + 128:(h + 1) * QK_PAD] = kpe16
            v_ref[:, h * 128:(h + 1) * 128] = _dot(ckv16, wuv_ref[:, h * 128:(h + 1) * 128]).astype(BF16)
    if not expand_kv:
        kf_ref[...] = jnp.zeros(kf_ref.shape, kf_ref.dtype)
        v_ref[...] = jnp.zeros(v_ref.shape, v_ref.dtype)


def _mla_qkv(proj, cos_t, sin_t, qg, kvg, wm, ws, wuk, wuv, tm, expand_kv):
    t = proj.shape[0]
    kf_shape = (t, HEADS * QK_PAD) if expand_kv else (8, 128)
    v_shape = (t, HEADS * V_DIM) if expand_kv else (8, 128)
    kf_spec = (pl.BlockSpec((tm, HEADS * QK_PAD), lambda i: (i, 0)) if expand_kv
               else pl.BlockSpec((8, 128), lambda i: (0, 0)))
    v_spec = (pl.BlockSpec((tm, HEADS * V_DIM), lambda i: (i, 0)) if expand_kv
              else pl.BlockSpec((8, 128), lambda i: (0, 0)))
    const = lambda shape: pl.BlockSpec(shape, lambda i: (0, 0))
    return pl.pallas_call(
        functools.partial(_mla_qkv_body, expand_kv=expand_kv),
        grid=(t // tm,),
        in_specs=[pl.BlockSpec((tm, Q_LORA), lambda i: (i, 0)),
                  pl.BlockSpec((tm, KV_LORA), lambda i: (i, 1)),
                  pl.BlockSpec((tm, 256), lambda i: (i, PAIR_COL0 // 256)),
                  pl.BlockSpec((tm, 128), lambda i: (i, 0)),
                  pl.BlockSpec((tm, 128), lambda i: (i, 0)),
                  const((1, Q_LORA)), const((1, KV_LORA)),
                  const(wm.shape), const(ws.shape), const(wuk.shape), const(wuv.shape)],
        out_specs=[pl.BlockSpec((tm, HEADS * QK_PAD), lambda i: (i, 0)),
                   pl.BlockSpec((tm, KV_LORA), lambda i: (i, 0)),
                   pl.BlockSpec((tm, 128), lambda i: (i, 0)),
                   kf_spec, v_spec],
        out_shape=[jax.ShapeDtypeStruct((t, HEADS * QK_PAD), BF16),
                   jax.ShapeDtypeStruct((t, KV_LORA), F32),
                   jax.ShapeDtypeStruct((t, 128), F32),
                   jax.ShapeDtypeStruct(kf_shape, BF16),
                   jax.ShapeDtypeStruct(v_shape, BF16)],
        compiler_params=_cparams("arbitrary"),
        name="mla_qkv",
    )(proj, proj, proj, cos_t, sin_t, qg, kvg, wm, ws, wuk, wuv)


def _flash_body(q_ref, k_ref, v_ref, gate_ref, o_ref, *, tq, n_q, n_split):
    qi = pl.program_id(2)
    rows = tq // n_split

    def scores(q_parts, j, diagonal):
        if not diagonal:
            k = k_ref[0, j * tq:(j + 1) * tq, :]
            return [_dot_nt(qp, k) for qp in q_parts]
        return [_dot_nt(qp, k_ref[0, j * tq:j * tq + (i + 1) * rows, :]) for i, qp in enumerate(q_parts)]

    for v in range(n_q):
        @pl.when(qi == v)
        def _(v=v):
            q_parts = [q_ref[0, i * rows:(i + 1) * rows, :] for i in range(n_split)]
            m = [jnp.full((rows, 128), -jnp.inf, F32) for _ in range(n_split)]
            l = [jnp.zeros((rows, 128), F32) for _ in range(n_split)]
            acc = [jnp.zeros((rows, V_DIM), F32) for _ in range(n_split)]
            s_next = scores(q_parts, 0, v == 0)
            for j in range(v + 1):
                s_cur = s_next
                if j < v:
                    s_next = scores(q_parts, j + 1, j + 1 == v)
                for i in range(n_split):
                    s = s_cur[i]
                    n_keys = s.shape[1]
                    if j == v:
                        ri = lax.broadcasted_iota(jnp.int32, s.shape, 0) + i * rows
                        ci = lax.broadcasted_iota(jnp.int32, s.shape, 1)
                        s = jnp.where(ci <= ri, s, -jnp.inf)
                    m_new = jnp.maximum(m[i], jnp.max(s, axis=1, keepdims=True))
                    alpha = jnp.exp(m[i] - m_new)
                    e = jnp.exp(s - m_new[:, :1])
                    l[i] = alpha * l[i] + jnp.sum(e, axis=1, keepdims=True)
                    acc[i] = alpha * acc[i] + _dot(e.astype(BF16), v_ref[0, j * tq:j * tq + n_keys, :])
                    m[i] = m_new
            for i in range(n_split):
                rs = slice(i * rows, (i + 1) * rows)
                o_ref[0, rs, :] = (acc[i] / l[i] * _silu(gate_ref[0, rs, :])).astype(o_ref.dtype)


def _prompt_attention(q3, k3, v3, proj3, tq):
    b, l, _ = q3.shape
    return pl.pallas_call(
        functools.partial(_flash_body, tq=tq, n_q=l // tq, n_split=2),
        grid=(b, HEADS, l // tq),
        in_specs=[pl.BlockSpec((1, tq, QK_PAD), lambda bi, h, qi: (bi, qi, h)),
                  pl.BlockSpec((1, l, QK_PAD), lambda bi, h, qi: (bi, 0, h)),
                  pl.BlockSpec((1, l, V_DIM), lambda bi, h, qi: (bi, 0, h)),
                  pl.BlockSpec((1, tq, V_DIM), lambda bi, h, qi: (bi, qi, GATE_COL0 // V_DIM + h))],
        out_specs=pl.BlockSpec((1, tq, V_DIM), lambda bi, h, qi: (bi, qi, h)),
        out_shape=jax.ShapeDtypeStruct((b, l, HEADS * V_DIM), BF16),
        compiler_params=_cparams("parallel", "parallel", "arbitrary"),
        name="prompt_attention",
    )(q3, k3, v3, proj3)


def _head_mm_body(x_ref, w_ref, o_ref):
    o_ref[0] = _dot(x_ref[...], w_ref[0]).astype(o_ref.dtype)


def _absorb_queries(q, w_ukt):
    t = q.shape[0]
    return pl.pallas_call(
        _head_mm_body,
        grid=(HEADS,),
        in_specs=[pl.BlockSpec((t, NOPE), lambda h: (0, 2 * h)),
                  pl.BlockSpec((1, NOPE, KV_LORA), lambda h: (h, 0, 0))],
        out_specs=pl.BlockSpec((1, t, KV_LORA), lambda h: (h, 0, 0)),
        out_shape=jax.ShapeDtypeStruct((HEADS, t, KV_LORA), BF16),
        compiler_params=_cparams("parallel"),
        name="absorb_queries",
    )(q, w_ukt)


def _expand_values_body(x_ref, w_ref, gate_ref, o_ref):
    o_ref[...] = (_dot(x_ref[0], w_ref[0]) * _silu(gate_ref[...])).astype(o_ref.dtype)


def _expand_values(o_lat, w_uvh, proj):
    t = o_lat.shape[1]
    return pl.pallas_call(
        _expand_values_body,
        grid=(HEADS,),
        in_specs=[pl.BlockSpec((1, t, KV_LORA), lambda h: (h, 0, 0)),
                  pl.BlockSpec((1, KV_LORA, V_DIM), lambda h: (h, 0, 0)),
                  pl.BlockSpec((t, V_DIM), lambda h: (0, GATE_COL0 // V_DIM + h))],
        out_specs=pl.BlockSpec((t, V_DIM), lambda h: (0, h)),
        out_shape=jax.ShapeDtypeStruct((t, HEADS * V_DIM), BF16),
        compiler_params=_cparams("parallel"),
        name="expand_values",
    )(o_lat, w_uvh, proj)


def _paged_attn_body(pt_ref, ql_ref, qr_ref, cn_ref, rn_ref, *rest, dec_seq):
    npg = PAGES_PER_STEP
    c_refs, r_refs = rest[:npg], rest[npg:2 * npg]
    o_ref, m_ref, l_ref, acc_ref, cbuf_even, sbuf_even, cbuf_odd, sbuf_odd = rest[2 * npg:]
    j = pl.program_id(1)
    last = pl.num_programs(1) - 1
    ql = ql_ref[0]
    qr = qr_ref[0]

    def fold(s, values):
        m_prev = m_ref[...]
        m_new = jnp.maximum(m_prev, jnp.max(s, axis=1, keepdims=True))
        alpha = jnp.exp(m_prev - m_new)
        e = jnp.exp(s - m_new[:, :1])
        l_ref[...] = alpha * l_ref[...] + jnp.sum(e, axis=1, keepdims=True)
        acc_ref[...] = alpha[:, :1] * acc_ref[...] + _dot(e.astype(BF16), values)
        m_ref[...] = m_new

    @pl.when(j == 0)
    def _():
        m_ref[...] = jnp.full(m_ref.shape, -jnp.inf, F32)
        l_ref[...] = jnp.zeros(l_ref.shape, F32)
        acc_ref[...] = jnp.zeros(acc_ref.shape, F32)
        c = cn_ref[0].astype(BF16)
        s = _dot_nt(ql, c) + _dot_nt(qr, rn_ref[0].astype(BF16))
        ri = lax.broadcasted_iota(jnp.int32, s.shape, 0)
        ci = lax.broadcasted_iota(jnp.int32, s.shape, 1)
        fold(jnp.where(ci <= ri % dec_seq, s, -jnp.inf), c)
        sbuf_odd[...] = jnp.full(sbuf_odd.shape, -jnp.inf, F32)
        cbuf_odd[...] = jnp.zeros(cbuf_odd.shape, BF16)

    for parity, (cbuf, sbuf, cbuf_prev, sbuf_prev) in enumerate(((cbuf_even, sbuf_even, cbuf_odd, sbuf_odd),
                                                                 (cbuf_odd, sbuf_odd, cbuf_even, sbuf_even))):
        @pl.when(j % 2 == parity)
        def _(cbuf=cbuf, sbuf=sbuf, cbuf_prev=cbuf_prev, sbuf_prev=sbuf_prev):
            for i, (c_ref, r_ref) in enumerate(zip(c_refs, r_refs)):
                c = c_ref[0, 0].astype(BF16)
                cbuf[i * PAGE_SIZE:(i + 1) * PAGE_SIZE, :] = c
                sbuf[:, i * PAGE_SIZE:(i + 1) * PAGE_SIZE] = _dot_nt(ql, c) + _dot(qr, r_ref[0, 0].astype(BF16))
            fold(sbuf_prev[...], cbuf_prev[...])

            @pl.when(j == last)
            def _():
                fold(sbuf[...], cbuf[...])
                o_ref[0] = (acc_ref[...] / l_ref[...][:, :1]).astype(o_ref.dtype)


def _paged_attention(q_lat, q_r, ckv_new, kpe_new, cache_ckv, cache_kpe_t, page_table_flat, n_pages, dec_seq):
    b, rows, _ = q_lat.shape
    npg = PAGES_PER_STEP

    def page_spec(shape, i):
        return pl.BlockSpec((1, 1) + shape, lambda bi, j, pt, i=i: (0, pt[bi * n_pages + j * npg + i], 0, 0))

    grid_spec = pltpu.PrefetchScalarGridSpec(
        num_scalar_prefetch=1,
        grid=(b, n_pages // npg),
        in_specs=([pl.BlockSpec((1, rows, KV_LORA), lambda bi, j, pt: (bi, 0, 0)),
                   pl.BlockSpec((1, rows, ROPE), lambda bi, j, pt: (bi, 0, 0)),
                   pl.BlockSpec((1, PAGE_SIZE, KV_LORA), lambda bi, j, pt: (bi, 0, 0)),
                   pl.BlockSpec((1, PAGE_SIZE, ROPE), lambda bi, j, pt: (bi, 0, 0))]
                  + [page_spec((PAGE_SIZE, KV_LORA), i) for i in range(npg)]
                  + [page_spec((ROPE, PAGE_SIZE), i) for i in range(npg)]),
        out_specs=pl.BlockSpec((1, rows, KV_LORA), lambda bi, j, pt: (bi, 0, 0)),
        scratch_shapes=[pltpu.VMEM((rows, 128), F32), pltpu.VMEM((rows, 128), F32),
                        pltpu.VMEM((rows, KV_LORA), F32),
                        pltpu.VMEM((npg * PAGE_SIZE, KV_LORA), BF16),
                        pltpu.VMEM((rows, npg * PAGE_SIZE), F32),
                        pltpu.VMEM((npg * PAGE_SIZE, KV_LORA), BF16),
                        pltpu.VMEM((rows, npg * PAGE_SIZE), F32)],
    )
    return pl.pallas_call(
        functools.partial(_paged_attn_body, dec_seq=dec_seq),
        grid_spec=grid_spec,
        out_shape=jax.ShapeDtypeStruct((b, rows, KV_LORA), BF16),
        compiler_params=_cparams("parallel", "arbitrary"),
        name="paged_attention",
    )(page_table_flat, q_lat, q_r, ckv_new, kpe_new, *([cache_ckv] * npg), *([cache_kpe_t] * npg))


def _prepare_weights(norm_g, ple_norm_g, w_ple_gate, w_ple_proj, final_norm_g, w_in_even, w_out_even, conv_a_w,
                     conv_qkv_w, dn_a_log, dn_dt_bias, dn_norm_g, w_in_mla, mla_q_norm_g, mla_kv_norm_g, w_uq,
                     w_uk, w_uv, w_o_mla):
    d = D_MODEL
    w = {}
    w["norm_g"] = norm_g.reshape(2, 1, d)
    w["ple_norm_g"] = ple_norm_g.reshape(2, 1, d)
    w["final_norm_g"] = final_norm_g.reshape(1, d)
    w["w_ple_gate"] = w_ple_gate.astype(BF16)
    w["w_ple_proj"] = w_ple_proj.astype(BF16)
    w_in = w_in_even[0]
    w["w_in_main"] = w_in[:, :8 * d].astype(BF16)
    w["w_in_gates"] = jnp.pad(w_in[:, 8 * d:], ((0, 0), (0, 128 - 2 * HEADS))).astype(BF16)
    w["a_log_row"] = jnp.pad(dn_a_log[0], (HEADS, 128 - 2 * HEADS)).reshape(1, 128)
    w["dt_bias_row"] = jnp.pad(dn_dt_bias[0], (HEADS, 128 - 2 * HEADS)).reshape(1, 128)
    w["w_out_a"] = w_out_even[0, :d].astype(BF16)
    w["w_out_b"] = w_out_even[0, d:].astype(BF16)
    w["conv_a_w8"] = jnp.pad(conv_a_w[0], ((0, 8 - SC_CONV), (0, 0)))
    w["conv_qkv_w8"] = jnp.pad(conv_qkv_w[0], ((0, 8 - DN_CONV), (0, 0)))
    w["dn_norm_g"] = dn_norm_g[0].reshape(1, HEAD_DIM)
    wi = w_in_mla[0]
    k_pe = wi[:, 2 * Q_LORA:2 * Q_LORA + ROPE]
    half = ROPE // 2
    zeros64 = jnp.zeros((d, 128 - ROPE), F32)
    w["w_in_mla"] = jnp.concatenate(
        [wi[:, :2 * Q_LORA], wi[:, 2 * Q_LORA + ROPE:], k_pe, zeros64, k_pe[:, half:], k_pe[:, :half], zeros64,
         jnp.zeros((d, MLA_IN_PAD - PAIR_COL0 - 256), F32)], axis=1).astype(BF16)
    w["mla_q_norm_g"] = mla_q_norm_g[0].reshape(1, Q_LORA)
    w["mla_kv_norm_g"] = mla_kv_norm_g[0].reshape(1, KV_LORA)
    uq = w_uq[0].reshape(Q_LORA, HEADS, NOPE + ROPE)
    x1, x2 = uq[:, :, NOPE:NOPE + half], uq[:, :, NOPE + half:]
    zpad = jnp.zeros((Q_LORA, HEADS, QK_PAD - NOPE - ROPE), F32)
    w["w_uq_main"] = jnp.concatenate([uq, zpad], axis=2).reshape(Q_LORA, HEADS * QK_PAD).astype(BF16)
    w["w_uq_swap"] = jnp.concatenate([x2, x1, zpad], axis=2).reshape(Q_LORA, HEADS * 128).astype(BF16)
    w["w_uk"] = w_uk[0].reshape(KV_LORA, HEADS * NOPE).astype(BF16)
    w["w_uv"] = w_uv[0].reshape(KV_LORA, HEADS * V_DIM).astype(BF16)
    w["w_uk_t"] = jnp.transpose(w_uk[0], (1, 2, 0)).astype(BF16)
    w["w_uv_h"] = jnp.transpose(w_uv[0], (1, 0, 2)).astype(BF16)
    w["w_o_mla"] = w_o_mla[0].astype(BF16)
    return w


def _rope_tables(pos):
    half = ROPE // 2
    inv = ROPE_THETA ** (-jnp.arange(half, dtype=F32) / half)
    ang = pos.astype(F32)[:, None] * inv
    cos, sin = jnp.cos(ang), jnp.sin(ang)
    z = jnp.zeros((pos.shape[0], 128 - ROPE), F32)
    return jnp.concatenate([cos, cos, z], axis=1), jnp.concatenate([-sin, sin, z], axis=1)


def _group_gates(gates3):
    b, l, _ = gates3.shape
    ngrp = HEADS // HEAD_GROUP
    beta = gates3[:, :, :HEADS].reshape(b, l, ngrp, HEAD_GROUP)
    g = gates3[:, :, HEADS:2 * HEADS].reshape(b, l, ngrp, HEAD_GROUP)
    grouped = jnp.transpose(jnp.concatenate([beta, g], axis=-1), (0, 2, 1, 3))
    return jnp.pad(grouped, ((0, 0), (0, 0), (0, 0), (0, 128 - 2 * HEAD_GROUP)))


def _layer_stack(x, p, pos, conv_a0, conv_qkv0, delta0, past, w, tm):
    b, l, d = x.shape
    t = b * l
    h = x.reshape(t, d)

    proj = _rms_matmul(h, w["norm_g"][0], w["w_in_main"], BF16 if past is None else F32, tm, 1024, "in_proj_even")
    gates = _rms_matmul_gates(h, w["norm_g"][0], w["w_in_gates"], w["a_log_row"], w["dt_bias_row"], tm)
    proj3 = proj.reshape(b, l, 8 * d)
    gates3 = gates.reshape(b, l, 128)
    conv_qkv1 = proj3[:, l - (DN_CONV - 1):l, 4 * d:7 * d].astype(F32)
    if past is None:
        y_a, a_last8 = _mixer_a(proj3, w["conv_a_w8"], 512, 512, False)
        y_b, delta1 = _delta_net(proj3, _group_gates(gates3), w["conv_qkv_w8"], None, w["dn_norm_g"], 256)
        conv_a1 = a_last8[:, 8 - (SC_CONV - 1):]
    else:
        n_state = DN_CONV - 1
        tail = SEG_ROWS - n_state - l
        a_state = jnp.pad(conv_a0, ((0, 0), (n_state - (SC_CONV - 1), 0), (0, 0)))
        a_ones = jnp.pad(jnp.ones((b, SC_CONV - 1, d), F32), ((0, 0), (n_state - (SC_CONV - 1), 0), (0, 0)))
        state_rows = jnp.concatenate([a_state, a_ones, jnp.zeros((b, n_state, 2 * d), F32), conv_qkv0,
                                      jnp.zeros((b, n_state, d), F32)], axis=2)
        stacked = jnp.concatenate([state_rows, proj3, jnp.zeros((b, tail, 8 * d), F32)], axis=1)
        stacked = stacked.reshape(1, b * SEG_ROWS, 8 * d)
        gates_s = jnp.pad(gates3, ((0, 0), (n_state, tail), (0, 0))).reshape(1, b * SEG_ROWS, 128)
        tl = min(b * SEG_ROWS, 512)
        y_a, prod = _mixer_a(stacked, w["conv_a_w8"], tl, 512, True)
        y_b, delta1 = _delta_net(stacked, _group_gates(gates_s), w["conv_qkv_w8"], delta0, w["dn_norm_g"], DN_CHUNK)
        tok = slice(n_state, n_state + l)
        y_a = y_a.reshape(b, SEG_ROWS, d)[:, tok]
        y_b = y_b.reshape(b, SEG_ROWS, d)[:, tok]
        conv_a1 = prod.reshape(b, SEG_ROWS, d)[:, n_state + l - (SC_CONV - 1):n_state + l]
    h = _matmul_residual([y_a.reshape(t, d), y_b.reshape(t, d)], [w["w_out_a"], w["w_out_b"]], h, min(tm, 512),
                         1024, "out_proj_even")
    p_all = p.reshape(p.shape[0] * t, PLE_DIM)
    h = _ple(h, w["ple_norm_g"][0], w["w_ple_gate"][0], p_all, 0, w["w_ple_proj"][0],
             w["final_norm_g"], False, min(tm, 512), "ple0")

    proj = _rms_matmul(h, w["norm_g"][1], w["w_in_mla"], F32, tm, 512, "in_proj_mla")
    cos_t, sin_t = _rope_tables(pos)
    cos_t = jnp.broadcast_to(cos_t[None], (b, l, 128)).reshape(t, 128)
    sin_t = jnp.broadcast_to(sin_t[None], (b, l, 128)).reshape(t, 128)
    q, ckv, kpe128, kfull, v = _mla_qkv(proj, cos_t, sin_t, w["mla_q_norm_g"], w["mla_kv_norm_g"], w["w_uq_main"],
                                        w["w_uq_swap"], w["w_uk"], w["w_uv"], min(t, 256), past is None)
    kpe = kpe128[:, :ROPE]
    if past is None:
        o = _prompt_attention(q.reshape(b, l, HEADS * QK_PAD), kfull.reshape(b, l, HEADS * QK_PAD),
                              v.reshape(b, l, HEADS * V_DIM), proj.reshape(b, l, MLA_IN_PAD), 512)
        o = o.reshape(t, HEADS * V_DIM)
    else:
        cache_ckv, cache_kpe, page_table = past
        n_pages = page_table.shape[1]
        q_lat = _absorb_queries(q, w["w_uk_t"])
        q_lat = jnp.transpose(q_lat.reshape(HEADS, b, l, KV_LORA), (1, 0, 2, 3)).reshape(b, HEADS * l, KV_LORA)
        q_r = q.reshape(b, l, HEADS, QK_PAD)[:, :, :, NOPE:NOPE + ROPE]
        q_r = jnp.transpose(q_r, (0, 2, 1, 3)).reshape(b, HEADS * l, ROPE)
        ckv_new = jnp.pad(ckv.reshape(b, l, KV_LORA), ((0, 0), (0, PAGE_SIZE - l), (0, 0)))
        kpe_new = jnp.pad(kpe.reshape(b, l, ROPE), ((0, 0), (0, PAGE_SIZE - l), (0, 0)))
        o_lat = _paged_attention(q_lat, q_r, ckv_new, kpe_new, cache_ckv, jnp.swapaxes(cache_kpe, 2, 3),
                                 page_table.reshape(-1), n_pages, l)
        o_lat = jnp.transpose(o_lat.reshape(b, HEADS, l, KV_LORA), (1, 0, 2, 3)).reshape(HEADS, t, KV_LORA)
        o = _expand_values(o_lat, w["w_uv_h"], proj)
    h = _matmul_residual([o], [w["w_o_mla"]], h, min(tm, 512), 1024, "out_proj_mla")
    y = _ple(h, w["ple_norm_g"][1], w["w_ple_gate"][1], p_all, 1, w["w_ple_proj"][1],
             w["final_norm_g"], True, min(tm, 512), "ple1")
    return (y.reshape(b, l, d), conv_a1[None], conv_qkv1[None], delta1[None],
            ckv.reshape(1, b, l, KV_LORA), kpe.reshape(1, b, l, ROPE))


def kernel(x_prompt, x_sample, cache_ckv, cache_kpe, state_conv_a, state_conv_qkv, state_delta, page_table,
           p_prompt, p_sample, norm_g, ple_norm_g, w_ple_gate, w_ple_proj, final_norm_g, w_in_even, w_out_even,
           conv_a_w, conv_qkv_w, dn_a_log, dn_dt_bias, dn_norm_g, w_in_mla, mla_q_norm_g, mla_kv_norm_g, w_uq,
           w_uk, w_uv, w_o_mla):
    w = _prepare_weights(norm_g, ple_norm_g, w_ple_gate, w_ple_proj, final_norm_g, w_in_even, w_out_even, conv_a_w,
                         conv_qkv_w, dn_a_log, dn_dt_bias, dn_norm_g, w_in_mla, mla_q_norm_g, mla_kv_norm_g, w_uq,
                         w_uk, w_uv, w_o_mla)
    seq = x_prompt.shape[1]
    dec_b, dec_l, _ = x_sample.shape
    past_len = page_table.shape[1] * PAGE_SIZE
    outs_p = _layer_stack(x_prompt, p_prompt, jnp.arange(seq), None, None, None, None, w, 1024)
    outs_s = _layer_stack(x_sample, p_sample, past_len + jnp.arange(dec_l), state_conv_a[0], state_conv_qkv[0],
                          state_delta[0], (cache_ckv, cache_kpe, page_table), w, dec_b * dec_l)
    return (outs_p[0], outs_s[0]) + tuple(outs_p[1:]) + tuple(outs_s[1:])
```

```python
import functools
import math

import jax
import jax.numpy as jnp
from jax import lax
from jax.experimental import pallas as pl
from jax.experimental.pallas import tpu as pltpu

F32 = jnp.float32
BF16 = jnp.bfloat16

NORM_EPS = 1e-6
D_MODEL = 2048
HEADS = 16
HEAD_DIM = 128
DN_CHUNK = 64
DN_CONV = 4
SC_CONV = 3
Q_LORA = 512
KV_LORA = 512
NOPE = 128
ROPE = 64
V_DIM = 128
PLE_DIM = 256
ROPE_THETA = 10000.0
PAGE_SIZE = 128
MLA_SCALE = (NOPE + ROPE) ** -0.5
QK_PAD = 256
MLA_IN_PAD = 3584
GATE_COL0 = Q_LORA + KV_LORA
PAIR_COL0 = GATE_COL0 + HEADS * V_DIM
HEAD_GROUP = 8
SEG_ROWS = 8
PAGES_PER_STEP = 32
SCORE_GROUP = 8

V7X_VMEM_LIMIT_BYTES = 56 * 1024 * 1024


def _cparams(*sem):
    return pltpu.CompilerParams(dimension_semantics=sem, vmem_limit_bytes=V7X_VMEM_LIMIT_BYTES)


def _sigmoid(x):
    return 1.0 / (1.0 + jnp.exp(-x))


def _silu(x):
    return x * _sigmoid(x)


def _rms_rows(x, g):
    return x * lax.rsqrt(jnp.mean(x * x, axis=-1, keepdims=True) + NORM_EPS) * g


def _dot(a, b):
    return jnp.dot(a, b, preferred_element_type=F32)


def _dot_nt(a, b):
    return lax.dot_general(a, b, (((1,), (1,)), ((), ())), preferred_element_type=F32)


def _rms_mm_body(x_ref, g_ref, w_ref, o_ref, xn_ref):
    @pl.when(pl.program_id(1) == 0)
    def _():
        xn_ref[...] = _rms_rows(x_ref[...], g_ref[...]).astype(BF16)

    o_ref[...] = _dot(xn_ref[...], w_ref[...]).astype(o_ref.dtype)


def _rms_matmul(x, g, w, out_dtype, tm, tn, name):
    t, k = x.shape
    n = w.shape[1]
    return pl.pallas_call(
        _rms_mm_body,
        grid=(t // tm, n // tn),
        in_specs=[pl.BlockSpec((tm, k), lambda i, j: (i, 0)),
                  pl.BlockSpec((1, k), lambda i, j: (0, 0)),
                  pl.BlockSpec((k, tn), lambda i, j: (0, j))],
        out_specs=pl.BlockSpec((tm, tn), lambda i, j: (i, j)),
        out_shape=jax.ShapeDtypeStruct((t, n), out_dtype),
        scratch_shapes=[pltpu.VMEM((tm, k), BF16)],
        compiler_params=_cparams("parallel", "arbitrary"),
        name=name,
    )(x, g, w)


def _rms_mm_gates_body(x_ref, g_ref, w_ref, alog_ref, dtb_ref, o_ref):
    xn = _rms_rows(x_ref[...], g_ref[...]).astype(BF16)
    y = _dot(xn, w_ref[...])
    lane = lax.broadcasted_iota(jnp.int32, y.shape, 1)
    beta = _sigmoid(y)
    t = y + dtb_ref[...]
    softplus = jnp.maximum(t, 0.0) + jnp.log1p(jnp.exp(-jnp.abs(t)))
    decay = -jnp.exp(alog_ref[...]) * softplus
    o_ref[...] = jnp.where(lane < HEADS, beta, decay)


def _rms_matmul_gates(x, g, w, alog_row, dtb_row, tm):
    t, k = x.shape
    return pl.pallas_call(
        _rms_mm_gates_body,
        grid=(t // tm,),
        in_specs=[pl.BlockSpec((tm, k), lambda i: (i, 0)),
                  pl.BlockSpec((1, k), lambda i: (0, 0)),
                  pl.BlockSpec((k, 128), lambda i: (0, 0)),
                  pl.BlockSpec((1, 128), lambda i: (0, 0)),
                  pl.BlockSpec((1, 128), lambda i: (0, 0))],
        out_specs=pl.BlockSpec((tm, 128), lambda i: (i, 0)),
        out_shape=jax.ShapeDtypeStruct((t, 128), F32),
        compiler_params=_cparams("parallel"),
        name="dn_gates",
    )(x, g, w, alog_row, dtb_row)


def _mm_res_body(*refs, n_in):
    xs, ws, res_ref, o_ref = refs[:n_in], refs[n_in:2 * n_in], refs[2 * n_in], refs[2 * n_in + 1]
    acc = res_ref[...]
    for x_ref, w_ref in zip(xs, ws):
        acc = acc + _dot(x_ref[...], w_ref[...])
    o_ref[...] = acc


def _matmul_residual(xs, ws, res, tm, tn, name):
    t, n = res.shape
    n_in = len(xs)
    in_specs = ([pl.BlockSpec((tm, x.shape[1]), lambda i, j: (i, 0)) for x in xs]
                + [pl.BlockSpec((w.shape[0], tn), lambda i, j: (0, j)) for w in ws]
                + [pl.BlockSpec((tm, tn), lambda i, j: (i, j))])
    return pl.pallas_call(
        functools.partial(_mm_res_body, n_in=n_in),
        grid=(t // tm, n // tn),
        in_specs=in_specs,
        out_specs=pl.BlockSpec((tm, tn), lambda i, j: (i, j)),
        out_shape=jax.ShapeDtypeStruct((t, n), F32),
        compiler_params=_cparams("parallel", "arbitrary"),
        name=name,
    )(*xs, *ws, res)


def _ple_body(h_ref, g_ref, wg_ref, p_ref, wp_ref, fg_ref, o_ref, *, final_norm):
    h = h_ref[...]
    xn = _rms_rows(h, g_ref[...]).astype(BF16)
    gate = _sigmoid(_dot(xn, wg_ref[...]))
    h2 = h + gate * _dot(p_ref[...].astype(BF16), wp_ref[...])
    if final_norm:
        h2 = _rms_rows(h2, fg_ref[...])
    o_ref[...] = h2


def _ple(h, g, w_gate, p_all, layer, w_proj, final_g, final_norm, tm, name):
    t, d = h.shape
    pd = p_all.shape[1]
    p_block0 = layer * (t // tm)
    return pl.pallas_call(
        functools.partial(_ple_body, final_norm=final_norm),
        grid=(t // tm,),
        in_specs=[pl.BlockSpec((tm, d), lambda i: (i, 0)),
                  pl.BlockSpec((1, d), lambda i: (0, 0)),
                  pl.BlockSpec((d, d), lambda i: (0, 0)),
                  pl.BlockSpec((tm, pd), lambda i: (p_block0 + i, 0)),
                  pl.BlockSpec((pd, d), lambda i: (0, 0)),
                  pl.BlockSpec((1, d), lambda i: (0, 0))],
        out_specs=pl.BlockSpec((tm, d), lambda i: (i, 0)),
        out_shape=jax.ShapeDtypeStruct((t, d), F32),
        compiler_params=_cparams("parallel"),
        name=name,
    )(h, g, w_gate, p_all, w_proj, final_g)


def _mixer_a_body(ac_ref, ah_ref, ab_ref, az_ref, w_ref, y_ref, st_ref, carry_ref, *, emit_prod):
    @pl.when(pl.program_id(2) == 0)
    def _():
        carry_ref[...] = jnp.zeros(carry_ref.shape, F32)

    prod = ac_ref[0].astype(F32) * ah_ref[0].astype(F32)
    tl = prod.shape[0]
    row = lax.broadcasted_iota(jnp.int32, prod.shape, 0)
    c6 = carry_ref[6:7, :]
    c7 = carry_ref[7:8, :]
    p1 = jnp.where(row == 0, c7, pltpu.roll(prod, 1, axis=0))
    p2 = jnp.where(row == 0, c6, jnp.where(row == 1, c7, pltpu.roll(prod, 2, axis=0)))
    w = w_ref[...]
    u = w[2:3, :] * prod + w[1:2, :] * p1 + w[0:1, :] * p2
    y_ref[0] = (ab_ref[0].astype(F32) * u * _silu(az_ref[0].astype(F32))).astype(y_ref.dtype)
    last = prod[tl - 8:, :]
    carry_ref[...] = last
    st_ref[0] = prod if emit_prod else last


def _mixer_a(proj3, conv_w8, tl, tc, emit_prod):
    b, l, _ = proj3.shape
    nct = D_MODEL // tc
    st_rows = tl if emit_prod else 8

    def col(gi):
        return pl.BlockSpec((1, tl, tc), lambda bi, ci, li, gi=gi: (bi, li, gi * nct + ci))

    return pl.pallas_call(
        functools.partial(_mixer_a_body, emit_prod=emit_prod),
        grid=(b, nct, l // tl),
        in_specs=[col(0), col(1), col(2), col(3),
                  pl.BlockSpec((8, tc), lambda bi, ci, li: (0, ci))],
        out_specs=[pl.BlockSpec((1, tl, tc), lambda bi, ci, li: (bi, li, ci)),
                   pl.BlockSpec((1, st_rows, tc), lambda bi, ci, li: (bi, li if emit_prod else 0, ci))],
        out_shape=[jax.ShapeDtypeStruct((b, l, D_MODEL), BF16),
                   jax.ShapeDtypeStruct((b, l if emit_prod else 8, D_MODEL), F32)],
        scratch_shapes=[pltpu.VMEM((8, tc), F32)],
        compiler_params=_cparams("parallel", "parallel", "arbitrary"),
        name="mixer_a",
    )(proj3, proj3, proj3, proj3, conv_w8)


def _segment_scan(x, seg):
    n = x.shape[0]
    row = lax.broadcasted_iota(jnp.int32, x.shape, 0)
    pos = row % seg
    cs = x
    s = 1
    while s < seg:
        cs = cs + jnp.where(pos >= s, pltpu.roll(cs, s, axis=0), 0.0)
        s *= 2
    if seg == n:
        return cs, jnp.broadcast_to(cs[n - 1:n, :], x.shape)
    tot = cs
    s = 1
    while s < seg:
        tot = jnp.where(pos + s < seg, pltpu.roll(tot, n - s, axis=0), tot)
        s *= 2
    return cs, tot


def _delta_prepare(q_ref, k_ref, v_ref, gb_ref, wq_ref, wk_ref, wv_ref, r0s, prev_rows, seg):
    c = DN_CHUNK
    items = [(n, r0, j) for n, r0 in enumerate(r0s) for j in range(HEAD_GROUP)]
    ri = lax.broadcasted_iota(jnp.int32, (c, c), 0)
    ci = lax.broadcasted_iota(jnp.int32, (c, c), 1)
    bcols, gcols, glasts = [], [], []
    for r0 in r0s:
        gates = gb_ref[0, 0, pl.ds(r0, c), :]
        gsum, gtot = _segment_scan(gates, seg)
        for j in range(HEAD_GROUP):
            bcols.append(jnp.broadcast_to(gates[:, j:j + 1], (c, HEAD_DIM)))
            gcols.append(jnp.broadcast_to(gsum[:, HEAD_GROUP + j:HEAD_GROUP + j + 1], (c, HEAD_DIM)))
            glasts.append(jnp.broadcast_to(gtot[:, HEAD_GROUP + j:HEAD_GROUP + j + 1], (c, HEAD_DIM)))

    def conv_silu(which, x_ref, w_ref, n, r0, j):
        lanes = slice(j * HEAD_DIM, (j + 1) * HEAD_DIM)
        cur = x_ref[0, pl.ds(r0, c), lanes].astype(F32)
        ext = jnp.concatenate([prev_rows(which, x_ref, lanes, n), cur], axis=0)
        w = w_ref[:, lanes]
        y = (w[3:4, :] * cur + w[2:3, :] * ext[7:7 + c, :] + w[1:2, :] * ext[6:6 + c, :]
             + w[0:1, :] * ext[5:5 + c, :])
        return _silu(y)

    qs = [conv_silu(0, q_ref, wq_ref, *item) for item in items]
    ks = [conv_silu(1, k_ref, wk_ref, *item) for item in items]
    vs = [conv_silu(2, v_ref, wv_ref, *item) for item in items]
    qs = [q * lax.rsqrt(jnp.sum(q * q, axis=-1, keepdims=True) + NORM_EPS) * (HEAD_DIM ** -0.5) for q in qs]
    ks = [k * lax.rsqrt(jnp.sum(k * k, axis=-1, keepdims=True) + NORM_EPS) for k in ks]
    kbs = [k * b for k, b in zip(ks, bcols)]
    m1s = [_dot_nt(jnp.concatenate([kb, q], axis=0).astype(BF16), k.astype(BF16))
           for kb, q, k in zip(kbs, qs, ks)]
    decays = []
    for gcol in gcols:
        grow = gcol.T[:c, :]
        decay = jnp.where(ri >= ci, jnp.exp(jnp.minimum(gcol[:, :c] - grow, 0.0)), 0.0)
        if seg < c:
            decay = jnp.where(ri // seg == ci // seg, decay, 0.0)
        decays.append(decay)
    powers = [jnp.where(ri > ci, m1[:c, :] * d, 0.0) for m1, d in zip(m1s, decays)]
    a_qks = [m1[c:, :] * d for m1, d in zip(m1s, decays)]
    t_invs = [jnp.where(ri == ci, 1.0, 0.0) - a for a in powers]
    span = 2
    while span < seg:
        p16s = [p.astype(BF16) for p in powers]
        powers = [_dot(p16, p16) for p16 in p16s]
        t_invs = [t + _dot(t.astype(BF16), p.astype(BF16)) for t, p in zip(t_invs, powers)]
        span *= 2
    egs = [jnp.exp(g) for g in gcols]
    sols = [_dot(t.astype(BF16), jnp.concatenate([v * b, kb * eg], axis=1).astype(BF16))
            for t, v, b, kb, eg in zip(t_invs, vs, bcols, kbs, egs)]
    k_dec_ts = [(k * jnp.exp(gl - g)).T for k, gl, g in zip(ks, glasts, gcols)]
    flat = [(sol[:, :HEAD_DIM], sol[:, HEAD_DIM:], q * eg, a_qk, k_dec_t, gl)
            for sol, q, eg, a_qk, k_dec_t, gl in zip(sols, qs, egs, a_qks, k_dec_ts, glasts)]
    return [flat[n * HEAD_GROUP:(n + 1) * HEAD_GROUP] for n in range(len(r0s))]


def _delta_seq_body(q_ref, k_ref, v_ref, z_ref, gb_ref, wq_ref, wk_ref, wv_ref, ng_ref, y_ref, s_ref,
                    s_scr, carry, u_scr, wq_scr, ak_scr, eg_scr):
    c = DN_CHUNK
    lb = q_ref.shape[1]
    n_chunks = lb // c

    @pl.when(pl.program_id(2) == 0)
    def _():
        s_scr[...] = jnp.zeros(s_scr.shape, F32)
        carry[...] = jnp.zeros(carry.shape, F32)

    heads = range(HEAD_GROUP)

    group = 2 if n_chunks % 2 == 0 else 1

    def prepare(gi, carry_val):
        r0s = [pl.multiple_of((gi * group + n) * c, c) for n in range(group)]

        def prev_rows(which, x_ref, lanes, n):
            if n > 0:
                return x_ref[0, pl.ds(pl.multiple_of(r0s[n] - 16, 16), 16), lanes].astype(F32)[8:, :]
            p0 = pl.multiple_of(jnp.maximum(r0s[0] - 16, 0), 16)
            return jnp.where(gi == 0, carry[which, :, lanes], x_ref[0, pl.ds(p0, 16), lanes].astype(F32)[8:, :])

        parts = _delta_prepare(q_ref, k_ref, v_ref, gb_ref, wq_ref, wk_ref, wv_ref, r0s, prev_rows, c)
        for n in range(group):
            ci = gi * group + n
            for j, (u, w, qe, a_qk, k_dec_t, glast) in enumerate(parts[n]):
                u_scr[ci, j] = u
                wq_scr[ci, j] = jnp.concatenate([w, qe], axis=0).astype(BF16)
                ak_scr[ci, j] = jnp.concatenate([a_qk, k_dec_t], axis=0).astype(BF16)
                eg_scr[ci, j] = jnp.exp(glast[:8, :])
        return carry_val

    lax.fori_loop(0, n_chunks // group, prepare, 0)

    def apply(ci, carry_val):
        r0 = pl.multiple_of(ci * c, c)
        s_prevs = [s_scr[j] for j in heads]
        r1s = [_dot(wq_scr[ci, j], s_prevs[j].astype(BF16)) for j in heads]
        v_news = [u_scr[ci, j] - r1s[j][:c, :] for j in heads]
        r2s = [_dot(ak_scr[ci, j], v_news[j].astype(BF16)) for j in heads]
        for j in heads:
            lanes = slice(j * HEAD_DIM, (j + 1) * HEAD_DIM)
            s_scr[j] = s_prevs[j] * eg_scr[ci, j][0:1, :] + r2s[j][c:, :]
            o = _rms_rows(r1s[j][c:, :] + r2s[j][:c, :], ng_ref[...]) * _silu(z_ref[0, pl.ds(r0, c), lanes].astype(F32))
            y_ref[0, pl.ds(r0, c), lanes] = o.astype(y_ref.dtype)
        return carry_val

    lax.fori_loop(0, n_chunks, apply, 0)
    carry[0] = q_ref[0, lb - 16:, :].astype(F32)[8:, :]
    carry[1] = k_ref[0, lb - 16:, :].astype(F32)[8:, :]
    carry[2] = v_ref[0, lb - 16:, :].astype(F32)[8:, :]
    s_ref[0] = s_scr[...]


def _delta_seg_body(q_ref, k_ref, v_ref, z_ref, gb_ref, wq_ref, wk_ref, wv_ref, ng_ref, s0_ref, y_ref, s_ref):
    c = DN_CHUNK
    seg = SEG_ROWS
    nseg = c // seg
    lb = q_ref.shape[1]
    zeros8 = jnp.zeros((8, HEAD_DIM), F32)
    lane_seg = lax.broadcasted_iota(jnp.int32, (HEAD_DIM, c), 1) // seg
    heads = range(HEAD_GROUP)
    for ci in range(lb // c):
        r0 = ci * c
        parts = _delta_prepare(q_ref, k_ref, v_ref, gb_ref, wq_ref, wk_ref, wv_ref, [r0],
                               lambda which, x_ref, lanes, n: zeros8, seg)[0]
        r1s = []
        for j, (u, w, qe, a_qk, k_dec_t, glast) in enumerate(parts):
            for p in range(nseg):
                rows = slice(p * seg, (p + 1) * seg)
                wq_p = jnp.concatenate([w[rows, :], qe[rows, :]], axis=0).astype(BF16)
                r1s.append(_dot(wq_p, s0_ref[ci * nseg + p, j].astype(BF16)))
        vn16s, o_firsts = [], []
        for j, (u, w, qe, a_qk, k_dec_t, glast) in enumerate(parts):
            mine = r1s[j * nseg:(j + 1) * nseg]
            v_new = u - jnp.concatenate([r1[:seg, :] for r1 in mine], axis=0)
            vn16s.append(v_new.astype(BF16))
            o_firsts.append(jnp.concatenate([r1[seg:, :] for r1 in mine], axis=0))
        o_seconds = [_dot(part[3].astype(BF16), vn16) for part, vn16 in zip(parts, vn16s)]
        for j, (u, w, qe, a_qk, k_dec_t, glast) in enumerate(parts):
            lanes = slice(j * HEAD_DIM, (j + 1) * HEAD_DIM)
            eg_last = jnp.exp(glast)
            for p in range(nseg):
                upd = _dot(jnp.where(lane_seg == p, k_dec_t, 0.0).astype(BF16), vn16s[j])
                s_ref[ci * nseg + p, j] = s0_ref[ci * nseg + p, j] * eg_last[p * seg:p * seg + 1, :] + upd
            o = _rms_rows(o_firsts[j] + o_seconds[j], ng_ref[...]) * _silu(z_ref[0, pl.ds(r0, c), lanes].astype(F32))
            y_ref[0, pl.ds(r0, c), lanes] = o.astype(y_ref.dtype)


def _delta_net(proj3, gates_g, conv_w8, s0, norm_g, lb):
    b, l, _ = proj3.shape
    gw = HEAD_GROUP * HEAD_DIM
    ngrp = HEADS // HEAD_GROUP
    per_d = D_MODEL // gw
    n_chunks = lb // DN_CHUNK

    def col(gi):
        return pl.BlockSpec((1, lb, gw), lambda bi, hi, li, gi=gi: (bi, li, gi * per_d + hi))

    def wcol(gi):
        return pl.BlockSpec((8, gw), lambda bi, hi, li, gi=gi: (0, gi * per_d + hi))

    in_specs = [col(4), col(5), col(6), col(7),
                pl.BlockSpec((1, 1, lb, 128), lambda bi, hi, li: (bi, hi, li, 0)),
                wcol(0), wcol(1), wcol(2),
                pl.BlockSpec((1, HEAD_DIM), lambda bi, hi, li: (0, 0))]
    y_spec = pl.BlockSpec((1, lb, gw), lambda bi, hi, li: (bi, li, hi))
    args = [proj3, proj3, proj3, proj3, gates_g, conv_w8, conv_w8, conv_w8, norm_g]
    if s0 is None:
        state_spec = pl.BlockSpec((1, HEAD_GROUP, HEAD_DIM, HEAD_DIM), lambda bi, hi, li: (bi, hi, 0, 0))
        return pl.pallas_call(
            _delta_seq_body,
            grid=(b, ngrp, l // lb),
            in_specs=in_specs,
            out_specs=[y_spec, state_spec],
            out_shape=[jax.ShapeDtypeStruct((b, l, D_MODEL), BF16),
                       jax.ShapeDtypeStruct((b, HEADS, HEAD_DIM, HEAD_DIM), F32)],
            scratch_shapes=[pltpu.VMEM((HEAD_GROUP, HEAD_DIM, HEAD_DIM), F32),
                            pltpu.VMEM((3, 8, gw), F32),
                            pltpu.VMEM((n_chunks, HEAD_GROUP, DN_CHUNK, HEAD_DIM), F32),
                            pltpu.VMEM((n_chunks, HEAD_GROUP, 2 * DN_CHUNK, HEAD_DIM), BF16),
                            pltpu.VMEM((n_chunks, HEAD_GROUP, DN_CHUNK + HEAD_DIM, DN_CHUNK), BF16),
                            pltpu.VMEM((n_chunks, HEAD_GROUP, 8, HEAD_DIM), F32)],
            compiler_params=_cparams("parallel", "parallel", "arbitrary"),
            name="delta_net_seq",
        )(*args)
    nseq = lb // SEG_ROWS
    state_spec = pl.BlockSpec((nseq, HEAD_GROUP, HEAD_DIM, HEAD_DIM), lambda bi, hi, li: (li, hi, 0, 0))
    return pl.pallas_call(
        _delta_seg_body,
        grid=(b, ngrp, l // lb),
        in_specs=in_specs + [state_spec],
        out_specs=[y_spec, state_spec],
        out_shape=[jax.ShapeDtypeStruct((b, l, D_MODEL), BF16),
                   jax.ShapeDtypeStruct(s0.shape, F32)],
        compiler_params=_cparams("parallel", "parallel", "parallel"),
        name="delta_net_seg",
    )(*args, s0)


def _mla_qkv_body(cq_ref, ckv_ref, pair_ref, cos_ref, sin_ref, qg_ref, kvg_ref, wm_ref, ws_ref, wuk_ref, wuv_ref,
                  q_ref, ckv_out, kpe_out, kf_ref, v_ref, *, expand_kv):
    cos_t = cos_ref[...]
    sin_t = sin_ref[...]
    cqn = _rms_rows(cq_ref[...], qg_ref[...]).astype(BF16)
    ckv = _rms_rows(ckv_ref[...], kvg_ref[...])
    ckv_out[...] = ckv
    pair = pair_ref[...]
    kpe = pair[:, :128] * cos_t + pair[:, 128:] * sin_t
    kpe_out[...] = kpe
    ckv16 = ckv.astype(BF16)
    kpe16 = kpe.astype(BF16)
    for h in range(HEADS):
        qm = _dot(cqn, wm_ref[:, h * QK_PAD:(h + 1) * QK_PAD])
        qs = _dot(cqn, ws_ref[:, h * 128:(h + 1) * 128])
        q_ref[:, h * QK_PAD:h * QK_PAD + 128] = (qm[:, :128] * MLA_SCALE).astype(BF16)
        q_ref[:, h * QK_PAD + 128:(h + 1) * QK_PAD] = ((qm[:, 128:] * cos_t + qs * sin_t) * MLA_SCALE).astype(BF16)
        if expand_kv:
            kf_ref[:, h * QK_PAD:h * QK_PAD + 128] = _dot(ckv16, wuk_ref[:, h * 128:(h + 1) * 128]).astype(BF16)
            kf_ref[:, h * QK_PAD + 128:(h + 1) * QK_PAD] = kpe16
            v_ref[:, h * 128:(h + 1) * 128] = _dot(ckv16, wuv_ref[:, h * 128:(h + 1) * 128]).astype(BF16)
    if not expand_kv:
        kf_ref[...] = jnp.zeros(kf_ref.shape, kf_ref.dtype)
        v_ref[...] = jnp.zeros(v_ref.shape, v_ref.dtype)


def _mla_qkv(proj, cos_t, sin_t, qg, kvg, wm, ws, wuk, wuv, tm, expand_kv):
    t = proj.shape[0]
    kf_shape = (t, HEADS * QK_PAD) if expand_kv else (8, 128)
    v_shape = (t, HEADS * V_DIM) if expand_kv else (8, 128)
    kf_spec = (pl.BlockSpec((tm, HEADS * QK_PAD), lambda i: (i, 0)) if expand_kv
               else pl.BlockSpec((8, 128), lambda i: (0, 0)))
    v_spec = (pl.BlockSpec((tm, HEADS * V_DIM), lambda i: (i, 0)) if expand_kv
              else pl.BlockSpec((8, 128), lambda i: (0, 0)))
    const = lambda shape: pl.BlockSpec(shape, lambda i: (0, 0))
    return pl.pallas_call(
        functools.partial(_mla_qkv_body, expand_kv=expand_kv),
        grid=(t // tm,),
        in_specs=[pl.BlockSpec((tm, Q_LORA), lambda i: (i, 0)),
                  pl.BlockSpec((tm, KV_LORA), lambda i: (i, 1)),
                  pl.BlockSpec((tm, 256), lambda i: (i, PAIR_COL0 // 256)),
                  pl.BlockSpec((tm, 128), lambda i: (i, 0)),
                  pl.BlockSpec((tm, 128), lambda i: (i, 0)),
                  const((1, Q_LORA)), const((1, KV_LORA)),
                  const(wm.shape), const(ws.shape), const(wuk.shape), const(wuv.shape)],
        out_specs=[pl.BlockSpec((tm, HEADS * QK_PAD), lambda i: (i, 0)),
                   pl.BlockSpec((tm, KV_LORA), lambda i: (i, 0)),
                   pl.BlockSpec((tm, 128), lambda i: (i, 0)),
                   kf_spec, v_spec],
        out_shape=[jax.ShapeDtypeStruct((t, HEADS * QK_PAD), BF16),
                   jax.ShapeDtypeStruct((t, KV_LORA), F32),
                   jax.ShapeDtypeStruct((t, 128), F32),
                   jax.ShapeDtypeStruct(kf_shape, BF16),
                   jax.ShapeDtypeStruct(v_shape, BF16)],
        compiler_params=_cparams("arbitrary"),
        name="mla_qkv",
    )(proj, proj, proj, cos_t, sin_t, qg, kvg, wm, ws, wuk, wuv)


def _flash_body(q_ref, k_ref, v_ref, gate_ref, o_ref, *, tq, n_q, n_split):
    qi = pl.program_id(2)
    rows = tq // n_split

    def scores(q_parts, j, diagonal):
        if not diagonal:
            k = k_ref[0, j * tq:(j + 1) * tq, :]
            return [_dot_nt(qp, k) for qp in q_parts]
        return [_dot_nt(qp, k_ref[0, j * tq:j * tq + (i + 1) * rows, :]) for i, qp in enumerate(q_parts)]

    for v in range(n_q):
        @pl.when(qi == v)
        def _(v=v):
            q_parts = [q_ref[0, i * rows:(i + 1) * rows, :] for i in range(n_split)]
            m = [jnp.full((rows, 128), -jnp.inf, F32) for _ in range(n_split)]
            l = [jnp.zeros((rows, 128), F32) for _ in range(n_split)]
            acc = [jnp.zeros((rows, V_DIM), F32) for _ in range(n_split)]
            s_next = scores(q_parts, 0, v == 0)
            for j in range(v + 1):
                s_cur = s_next
                if j < v:
                    s_next = scores(q_parts, j + 1, j + 1 == v)
                for i in range(n_split):
                    s = s_cur[i]
                    n_keys = s.shape[1]
                    if j == v:
                        ri = lax.broadcasted_iota(jnp.int32, s.shape, 0) + i * rows
                        ci = lax.broadcasted_iota(jnp.int32, s.shape, 1)
                        s = jnp.where(ci <= ri, s, -jnp.inf)
                    m_new = jnp.maximum(m[i], jnp.max(s, axis=1, keepdims=True))
                    alpha = jnp.exp(m[i] - m_new)
                    e = jnp.exp(s - m_new[:, :1])
                    l[i] = alpha * l[i] + jnp.sum(e, axis=1, keepdims=True)
                    acc[i] = alpha * acc[i] + _dot(e.astype(BF16), v_ref[0, j * tq:j * tq + n_keys, :])
                    m[i] = m_new
            for i in range(n_split):
                rs = slice(i * rows, (i + 1) * rows)
                o_ref[0, rs, :] = (acc[i] / l[i] * _silu(gate_ref[0, rs, :])).astype(o_ref.dtype)


def _prompt_attention(q3, k3, v3, proj3, tq):
    b, l, _ = q3.shape
    return pl.pallas_call(
        functools.partial(_flash_body, tq=tq, n_q=l // tq, n_split=2),
        grid=(b, HEADS, l // tq),
        in_specs=[pl.BlockSpec((1, tq, QK_PAD), lambda bi, h, qi: (bi, qi, h)),
                  pl.BlockSpec((1, l, QK_PAD), lambda bi, h, qi: (bi, 0, h)),
                  pl.BlockSpec((1, l, V_DIM), lambda bi, h, qi: (bi, 0, h)),
                  pl.BlockSpec((1, tq, V_DIM), lambda bi, h, qi: (bi, qi, GATE_COL0 // V_DIM + h))],
        out_specs=pl.BlockSpec((1, tq, V_DIM), lambda bi, h, qi: (bi, qi, h)),
        out_shape=jax.ShapeDtypeStruct((b, l, HEADS * V_DIM), BF16),
        compiler_params=_cparams("parallel", "parallel", "arbitrary"),
        name="prompt_attention",
    )(q3, k3, v3, proj3)


def _head_mm_body(x_ref, w_ref, o_ref):
    o_ref[0] = _dot(x_ref[...], w_ref[0]).astype(o_ref.dtype)


def _absorb_queries(q, w_ukt):
    t = q.shape[0]
    return pl.pallas_call(
        _head_mm_body,
        grid=(HEADS,),
        in_specs=[pl.BlockSpec((t, NOPE), lambda h: (0, 2 * h)),
                  pl.BlockSpec((1, NOPE, KV_LORA), lambda h: (h, 0, 0))],
        out_specs=pl.BlockSpec((1, t, KV_LORA), lambda h: (h, 0, 0)),
        out_shape=jax.ShapeDtypeStruct((HEADS, t, KV_LORA), BF16),
        compiler_params=_cparams("parallel"),
        name="absorb_queries",
    )(q, w_ukt)


def _expand_values_body(x_ref, w_ref, gate_ref, o_ref):
    o_ref[...] = (_dot(x_ref[0], w_ref[0]) * _silu(gate_ref[...])).astype(o_ref.dtype)


def _expand_values(o_lat, w_uvh, proj):
    t = o_lat.shape[1]
    return pl.pallas_call(
        _expand_values_body,
        grid=(HEADS,),
        in_specs=[pl.BlockSpec((1, t, KV_LORA), lambda h: (h, 0, 0)),
                  pl.BlockSpec((1, KV_LORA, V_DIM), lambda h: (h, 0, 0)),
                  pl.BlockSpec((t, V_DIM), lambda h: (0, GATE_COL0 // V_DIM + h))],
        out_specs=pl.BlockSpec((t, V_DIM), lambda h: (0, h)),
        out_shape=jax.ShapeDtypeStruct((t, HEADS * V_DIM), BF16),
        compiler_params=_cparams("parallel"),
        name="expand_values",
    )(o_lat, w_uvh, proj)


def _paged_attn_body(pt_ref, ql_ref, qlt_ref, qr_ref, cn_ref, rn_ref, *rest, dec_seq):
    npg = PAGES_PER_STEP
    c_refs, r_refs = rest[:npg], rest[npg:2 * npg]
    o_ref, m_ref, l_ref, acc_ref, cbuf_even, sbuf_even, cbuf_odd, sbuf_odd = rest[2 * npg:]
    j = pl.program_id(1)
    last = pl.num_programs(1) - 1
    ql = ql_ref[0]
    qlt = qlt_ref[0]
    qr = qr_ref[0]
    rows = ql.shape[0]

    def fold(s, values):
        m_prev = m_ref[...]
        m_new = jnp.maximum(m_prev, jnp.max(s, axis=1, keepdims=True))
        alpha = jnp.exp(m_prev - m_new)
        e = jnp.exp(s - m_new[:, :1])
        l_ref[...] = alpha * l_ref[...] + jnp.sum(e, axis=1, keepdims=True)
        acc_ref[...] = alpha[:, :1] * acc_ref[...] + _dot(e.astype(BF16), values)
        m_ref[...] = m_new

    @pl.when(j == 0)
    def _():
        m_ref[...] = jnp.full(m_ref.shape, -jnp.inf, F32)
        l_ref[...] = jnp.zeros(l_ref.shape, F32)
        acc_ref[...] = jnp.zeros(acc_ref.shape, F32)
        c = cn_ref[0].astype(BF16)
        s = _dot_nt(ql, c) + _dot_nt(qr, rn_ref[0].astype(BF16))
        ri = lax.broadcasted_iota(jnp.int32, s.shape, 0)
        ci = lax.broadcasted_iota(jnp.int32, s.shape, 1)
        fold(jnp.where(ci <= ri % dec_seq, s, -jnp.inf), c)
        sbuf_odd[...] = jnp.full(sbuf_odd.shape, -jnp.inf, F32)
        cbuf_odd[...] = jnp.zeros(cbuf_odd.shape, BF16)

    for parity, (cbuf, sbuf, cbuf_prev, sbuf_prev) in enumerate(((cbuf_even, sbuf_even, cbuf_odd, sbuf_odd),
                                                                 (cbuf_odd, sbuf_odd, cbuf_even, sbuf_even))):
        @pl.when(j % 2 == parity)
        def _(cbuf=cbuf, sbuf=sbuf, cbuf_prev=cbuf_prev, sbuf_prev=sbuf_prev):
            for i, c_ref in enumerate(c_refs):
                cbuf[i * PAGE_SIZE:(i + 1) * PAGE_SIZE, :] = c_ref[0, 0].astype(BF16)
            span = SCORE_GROUP * PAGE_SIZE
            for g in range(npg // SCORE_GROUP):
                keys = slice(g * span, (g + 1) * span)
                s_lat = _dot(cbuf[keys, :], qlt).T[:rows, :]
                s_rope = jnp.concatenate([_dot(qr, r_refs[g * SCORE_GROUP + i][0, 0].astype(BF16))
                                          for i in range(SCORE_GROUP)], axis=1)
                sbuf[:, keys] = s_lat + s_rope
            fold(sbuf_prev[...], cbuf_prev[...])

            @pl.when(j == last)
            def _():
                fold(sbuf[...], cbuf[...])
                o_ref[0] = (acc_ref[...] / l_ref[...][:, :1]).astype(o_ref.dtype)


def _paged_attention(q_lat, q_r, ckv_new, kpe_new, cache_ckv, cache_kpe_t, page_table_flat, n_pages, dec_seq):
    b, rows, _ = q_lat.shape
    npg = PAGES_PER_STEP
    q_lat_t = jnp.pad(jnp.swapaxes(q_lat, 1, 2), ((0, 0), (0, 0), (0, 128 - rows)))

    def page_spec(shape, i):
        return pl.BlockSpec((1, 1) + shape, lambda bi, j, pt, i=i: (0, pt[bi * n_pages + j * npg + i], 0, 0))

    grid_spec = pltpu.PrefetchScalarGridSpec(
        num_scalar_prefetch=1,
        grid=(b, n_pages // npg),
        in_specs=([pl.BlockSpec((1, rows, KV_LORA), lambda bi, j, pt: (bi, 0, 0)),
                   pl.BlockSpec((1, KV_LORA, 128), lambda bi, j, pt: (bi, 0, 0)),
                   pl.BlockSpec((1, rows, ROPE), lambda bi, j, pt: (bi, 0, 0)),
                   pl.BlockSpec((1, PAGE_SIZE, KV_LORA), lambda bi, j, pt: (bi, 0, 0)),
                   pl.BlockSpec((1, PAGE_SIZE, ROPE), lambda bi, j, pt: (bi, 0, 0))]
                  + [page_spec((PAGE_SIZE, KV_LORA), i) for i in range(npg)]
                  + [page_spec((ROPE, PAGE_SIZE), i) for i in range(npg)]),
        out_specs=pl.BlockSpec((1, rows, KV_LORA), lambda bi, j, pt: (bi, 0, 0)),
        scratch_shapes=[pltpu.VMEM((rows, 128), F32), pltpu.VMEM((rows, 128), F32),
                        pltpu.VMEM((rows, KV_LORA), F32),
                        pltpu.VMEM((npg * PAGE_SIZE, KV_LORA), BF16),
                        pltpu.VMEM((rows, npg * PAGE_SIZE), F32),
                        pltpu.VMEM((npg * PAGE_SIZE, KV_LORA), BF16),
                        pltpu.VMEM((rows, npg * PAGE_SIZE), F32)],
    )
    return pl.pallas_call(
        functools.partial(_paged_attn_body, dec_seq=dec_seq),
        grid_spec=grid_spec,
        out_shape=jax.ShapeDtypeStruct((b, rows, KV_LORA), BF16),
        compiler_params=_cparams("parallel", "arbitrary"),
        name="paged_attention",
    )(page_table_flat, q_lat, q_lat_t, q_r, ckv_new, kpe_new, *([cache_ckv] * npg), *([cache_kpe_t] * npg))


def _prepare_weights(norm_g, ple_norm_g, w_ple_gate, w_ple_proj, final_norm_g, w_in_even, w_out_even, conv_a_w,
                     conv_qkv_w, dn_a_log, dn_dt_bias, dn_norm_g, w_in_mla, mla_q_norm_g, mla_kv_norm_g, w_uq,
                     w_uk, w_uv, w_o_mla):
    d = D_MODEL
    w = {}
    w["norm_g"] = norm_g.reshape(2, 1, d)
    w["ple_norm_g"] = ple_norm_g.reshape(2, 1, d)
    w["final_norm_g"] = final_norm_g.reshape(1, d)
    w["w_ple_gate"] = w_ple_gate.astype(BF16)
    w["w_ple_proj"] = w_ple_proj.astype(BF16)
    w_in = w_in_even[0]
    w["w_in_main"] = w_in[:, :8 * d].astype(BF16)
    w["w_in_gates"] = jnp.pad(w_in[:, 8 * d:], ((0, 0), (0, 128 - 2 * HEADS))).astype(BF16)
    w["a_log_row"] = jnp.pad(dn_a_log[0], (HEADS, 128 - 2 * HEADS)).reshape(1, 128)
    w["dt_bias_row"] = jnp.pad(dn_dt_bias[0], (HEADS, 128 - 2 * HEADS)).reshape(1, 128)
    w["w_out_a"] = w_out_even[0, :d].astype(BF16)
    w["w_out_b"] = w_out_even[0, d:].astype(BF16)
    w["conv_a_w8"] = jnp.pad(conv_a_w[0], ((0, 8 - SC_CONV), (0, 0)))
    w["conv_qkv_w8"] = jnp.pad(conv_qkv_w[0], ((0, 8 - DN_CONV), (0, 0)))
    w["dn_norm_g"] = dn_norm_g[0].reshape(1, HEAD_DIM)
    wi = w_in_mla[0]
    k_pe = wi[:, 2 * Q_LORA:2 * Q_LORA + ROPE]
    half = ROPE // 2
    zeros64 = jnp.zeros((d, 128 - ROPE), F32)
    w["w_in_mla"] = jnp.concatenate(
        [wi[:, :2 * Q_LORA], wi[:, 2 * Q_LORA + ROPE:], k_pe, zeros64, k_pe[:, half:], k_pe[:, :half], zeros64,
         jnp.zeros((d, MLA_IN_PAD - PAIR_COL0 - 256), F32)], axis=1).astype(BF16)
    w["mla_q_norm_g"] = mla_q_norm_g[0].reshape(1, Q_LORA)
    w["mla_kv_norm_g"] = mla_kv_norm_g[0].reshape(1, KV_LORA)
    uq = w_uq[0].reshape(Q_LORA, HEADS, NOPE + ROPE)
    x1, x2 = uq[:, :, NOPE:NOPE + half], uq[:, :, NOPE + half:]
    zpad = jnp.zeros((Q_LORA, HEADS, QK_PAD - NOPE - ROPE), F32)
    w["w_uq_main"] = jnp.concatenate([uq, zpad], axis=2).reshape(Q_LORA, HEADS * QK_PAD).astype(BF16)
    w["w_uq_swap"] = jnp.concatenate([x2, x1, zpad], axis=2).reshape(Q_LORA, HEADS * 128).astype(BF16)
    w["w_uk"] = w_uk[0].reshape(KV_LORA, HEADS * NOPE).astype(BF16)
    w["w_uv"] = w_uv[0].reshape(KV_LORA, HEADS * V_DIM).astype(BF16)
    w["w_uk_t"] = jnp.transpose(w_uk[0], (1, 2, 0)).astype(BF16)
    w["w_uv_h"] = jnp.transpose(w_uv[0], (1, 0, 2)).astype(BF16)
    w["w_o_mla"] = w_o_mla[0].astype(BF16)
    return w


def _rope_tables(pos):
    half = ROPE // 2
    inv = ROPE_THETA ** (-jnp.arange(half, dtype=F32) / half)
    ang = pos.astype(F32)[:, None] * inv
    cos, sin = jnp.cos(ang), jnp.sin(ang)
    z = jnp.zeros((pos.shape[0], 128 - ROPE), F32)
    return jnp.concatenate([cos, cos, z], axis=1), jnp.concatenate([-sin, sin, z], axis=1)


def _group_gates(gates3):
    b, l, _ = gates3.shape
    ngrp = HEADS // HEAD_GROUP
    beta = gates3[:, :, :HEADS].reshape(b, l, ngrp, HEAD_GROUP)
    g = gates3[:, :, HEADS:2 * HEADS].reshape(b, l, ngrp, HEAD_GROUP)
    grouped = jnp.transpose(jnp.concatenate([beta, g], axis=-1), (0, 2, 1, 3))
    return jnp.pad(grouped, ((0, 0), (0, 0), (0, 0), (0, 128 - 2 * HEAD_GROUP)))


def _layer_stack(x, p, pos, conv_a0, conv_qkv0, delta0, past, w, tm):
    b, l, d = x.shape
    t = b * l
    h = x.reshape(t, d)

    proj = _rms_matmul(h, w["norm_g"][0], w["w_in_main"], BF16 if past is None else F32, tm, 2048, "in_proj_even")
    gates = _rms_matmul_gates(h, w["norm_g"][0], w["w_in_gates"], w["a_log_row"], w["dt_bias_row"], tm)
    proj3 = proj.reshape(b, l, 8 * d)
    gates3 = gates.reshape(b, l, 128)
    conv_qkv1 = proj3[:, l - (DN_CONV - 1):l, 4 * d:7 * d].astype(F32)
    if past is None:
        y_a, a_last8 = _mixer_a(proj3, w["conv_a_w8"], 512, 512, False)
        y_b, delta1 = _delta_net(proj3, _group_gates(gates3), w["conv_qkv_w8"], None, w["dn_norm_g"], 256)
        conv_a1 = a_last8[:, 8 - (SC_CONV - 1):]
    else:
        n_state = DN_CONV - 1
        tail = SEG_ROWS - n_state - l
        a_state = jnp.pad(conv_a0, ((0, 0), (n_state - (SC_CONV - 1), 0), (0, 0)))
        a_ones = jnp.pad(jnp.ones((b, SC_CONV - 1, d), F32), ((0, 0), (n_state - (SC_CONV - 1), 0), (0, 0)))
        state_rows = jnp.concatenate([a_state, a_ones, jnp.zeros((b, n_state, 2 * d), F32), conv_qkv0,
                                      jnp.zeros((b, n_state, d), F32)], axis=2)
        stacked = jnp.concatenate([state_rows, proj3, jnp.zeros((b, tail, 8 * d), F32)], axis=1)
        stacked = stacked.reshape(1, b * SEG_ROWS, 8 * d)
        gates_s = jnp.pad(gates3, ((0, 0), (n_state, tail), (0, 0))).reshape(1, b * SEG_ROWS, 128)
        tl = min(b * SEG_ROWS, 512)
        y_a, prod = _mixer_a(stacked, w["conv_a_w8"], tl, 512, True)
        y_b, delta1 = _delta_net(stacked, _group_gates(gates_s), w["conv_qkv_w8"], delta0, w["dn_norm_g"], DN_CHUNK)
        tok = slice(n_state, n_state + l)
        y_a = y_a.reshape(b, SEG_ROWS, d)[:, tok]
        y_b = y_b.reshape(b, SEG_ROWS, d)[:, tok]
        conv_a1 = prod.reshape(b, SEG_ROWS, d)[:, n_state + l - (SC_CONV - 1):n_state + l]
    h = _matmul_residual([y_a.reshape(t, d), y_b.reshape(t, d)], [w["w_out_a"], w["w_out_b"]], h, min(tm, 512),
                         1024, "out_proj_even")
    p_all = p.reshape(p.shape[0] * t, PLE_DIM)
    h = _ple(h, w["ple_norm_g"][0], w["w_ple_gate"][0], p_all, 0, w["w_ple_proj"][0],
             w["final_norm_g"], False, min(tm, 512), "ple0")

    proj = _rms_matmul(h, w["norm_g"][1], w["w_in_mla"], F32, tm, 512, "in_proj_mla")
    cos_t, sin_t = _rope_tables(pos)
    cos_t = jnp.broadcast_to(cos_t[None], (b, l, 128)).reshape(t, 128)
    sin_t = jnp.broadcast_to(sin_t[None], (b, l, 128)).reshape(t, 128)
    q, ckv, kpe128, kfull, v = _mla_qkv(proj, cos_t, sin_t, w["mla_q_norm_g"], w["mla_kv_norm_g"], w["w_uq_main"],
                                        w["w_uq_swap"], w["w_uk"], w["w_uv"], min(t, 256), past is None)
    kpe = kpe128[:, :ROPE]
    if past is None:
        o = _prompt_attention(q.reshape(b, l, HEADS * QK_PAD), kfull.reshape(b, l, HEADS * QK_PAD),
                              v.reshape(b, l, HEADS * V_DIM), proj.reshape(b, l, MLA_IN_PAD), 512)
        o = o.reshape(t, HEADS * V_DIM)
    else:
        cache_ckv, cache_kpe, page_table = past
        n_pages = page_table.shape[1]
        q_lat = _absorb_queries(q, w["w_uk_t"])
        q_lat = jnp.transpose(q_lat.reshape(HEADS, b, l, KV_LORA), (1, 0, 2, 3)).reshape(b, HEADS * l, KV_LORA)
        q_r = q.reshape(b, l, HEADS, QK_PAD)[:, :, :, NOPE:NOPE + ROPE]
        q_r = jnp.transpose(q_r, (0, 2, 1, 3)).reshape(b, HEADS * l, ROPE)
        ckv_new = jnp.pad(ckv.reshape(b, l, KV_LORA), ((0, 0), (0, PAGE_SIZE - l), (0, 0)))
        kpe_new = jnp.pad(kpe.reshape(b, l, ROPE), ((0, 0), (0, PAGE_SIZE - l), (0, 0)))
        o_lat = _paged_attention(q_lat, q_r, ckv_new, kpe_new, cache_ckv, jnp.swapaxes(cache_kpe, 2, 3),
                                 page_table.reshape(-1), n_pages, l)
        o_lat = jnp.transpose(o_lat.reshape(b, HEADS, l, KV_LORA), (1, 0, 2, 3)).reshape(HEADS, t, KV_LORA)
        o = _expand_values(o_lat, w["w_uv_h"], proj)
    h = _matmul_residual([o], [w["w_o_mla"]], h, min(tm, 512), 1024, "out_proj_mla")
    y = _ple(h, w["ple_norm_g"][1], w["w_ple_gate"][1], p_all, 1, w["w_ple_proj"][1],
             w["final_norm_g"], True, min(tm, 512), "ple1")
    return (y.reshape(b, l, d), conv_a1[None], conv_qkv1[None], delta1[None],
            ckv.reshape(1, b, l, KV_LORA), kpe.reshape(1, b, l, ROPE))


def kernel(x_prompt, x_sample, cache_ckv, cache_kpe, state_conv_a, state_conv_qkv, state_delta, page_table,
           p_prompt, p_sample, norm_g, ple_norm_g, w_ple_gate, w_ple_proj, final_norm_g, w_in_even, w_out_even,
           conv_a_w, conv_qkv_w, dn_a_log, dn_dt_bias, dn_norm_g, w_in_mla, mla_q_norm_g, mla_kv_norm_g, w_uq,
           w_uk, w_uv, w_o_mla):
    w = _prepare_weights(norm_g, ple_norm_g, w_ple_gate, w_ple_proj, final_norm_g, w_in_even, w_out_even, conv_a_w,
                         conv_qkv_w, dn_a_log, dn_dt_bias, dn_norm_g, w_in_mla, mla_q_norm_g, mla_kv_norm_g, w_uq,
                         w_uk, w_uv, w_o_mla)
    seq = x_prompt.shape[1]
    dec_b, dec_l, _ = x_sample.shape
    past_len = page_table.shape[1] * PAGE_SIZE
    outs_p = _layer_stack(x_prompt, p_prompt, jnp.arange(seq), None, None, None, None, w, 1024)
    outs_s = _layer_stack(x_sample, p_sample, past_len + jnp.arange(dec_l), state_conv_a[0], state_conv_qkv[0],
                          state_delta[0], (cache_ckv, cache_kpe, page_table), w, dec_b * dec_l)
    return (outs_p[0], outs_s[0]) + tuple(outs_p[1:]) + tuple(outs_s[1:])
```

```python
import functools
import math

import jax
import jax.numpy as jnp
from jax import lax
from jax.experimental import pallas as pl
from jax.experimental.pallas import tpu as pltpu

F32 = jnp.float32
BF16 = jnp.bfloat16

NORM_EPS = 1e-6
D_MODEL = 2048
HEADS = 16
HEAD_DIM = 128
DN_CHUNK = 64
DN_CONV = 4
SC_CONV = 3
Q_LORA = 512
KV_LORA = 512
NOPE = 128
ROPE = 64
V_DIM = 128
PLE_DIM = 256
ROPE_THETA = 10000.0
PAGE_SIZE = 128
MLA_SCALE = (NOPE + ROPE) ** -0.5
QK_PAD = 256
MLA_IN_PAD = 3584
GATE_COL0 = Q_LORA + KV_LORA
PAIR_COL0 = GATE_COL0 + HEADS * V_DIM
HEAD_GROUP = 8
SEG_ROWS = 8
PAGES_PER_STEP = 32
SCORE_GROUP = 8

V7X_VMEM_LIMIT_BYTES = 56 * 1024 * 1024


def _cparams(*sem):
    return pltpu.CompilerParams(dimension_semantics=sem, vmem_limit_bytes=V7X_VMEM_LIMIT_BYTES)


def _sigmoid(x):
    return 1.0 / (1.0 + jnp.exp(-x))


def _silu(x):
    return x * _sigmoid(x)


def _rms_rows(x, g):
    return x * lax.rsqrt(jnp.mean(x * x, axis=-1, keepdims=True) + NORM_EPS) * g


def _dot(a, b):
    return jnp.dot(a, b, preferred_element_type=F32)


def _dot_nt(a, b):
    return lax.dot_general(a, b, (((1,), (1,)), ((), ())), preferred_element_type=F32)


def _rms_mm_body(x_ref, g_ref, w_ref, o_ref, xn_ref):
    @pl.when(pl.program_id(1) == 0)
    def _():
        xn_ref[...] = _rms_rows(x_ref[...], g_ref[...]).astype(BF16)

    o_ref[...] = _dot(xn_ref[...], w_ref[...]).astype(o_ref.dtype)


def _rms_matmul(x, g, w, out_dtype, tm, tn, name, n=None):
    t, k = x.shape
    n = w.shape[1] if n is None else n
    return pl.pallas_call(
        _rms_mm_body,
        grid=(t // tm, n // tn),
        in_specs=[pl.BlockSpec((tm, k), lambda i, j: (i, 0)),
                  pl.BlockSpec((1, k), lambda i, j: (0, 0)),
                  pl.BlockSpec((k, tn), lambda i, j: (0, j))],
        out_specs=pl.BlockSpec((tm, tn), lambda i, j: (i, j)),
        out_shape=jax.ShapeDtypeStruct((t, n), out_dtype),
        scratch_shapes=[pltpu.VMEM((tm, k), BF16)],
        compiler_params=_cparams("parallel", "arbitrary"),
        name=name,
    )(x, g, w)


def _rms_mm_gates_body(x_ref, g_ref, w_ref, alog_ref, dtb_ref, o_ref):
    xn = _rms_rows(x_ref[...], g_ref[...]).astype(BF16)
    y = _dot(xn, w_ref[...])
    lane = lax.broadcasted_iota(jnp.int32, y.shape, 1)
    beta = _sigmoid(y)
    t = y + dtb_ref[...]
    softplus = jnp.maximum(t, 0.0) + jnp.log1p(jnp.exp(-jnp.abs(t)))
    decay = -jnp.exp(alog_ref[...]) * softplus
    o_ref[...] = jnp.where(lane < HEADS, beta, decay)


def _rms_matmul_gates(x, g, w, alog_row, dtb_row, tm):
    t, k = x.shape
    return pl.pallas_call(
        _rms_mm_gates_body,
        grid=(t // tm,),
        in_specs=[pl.BlockSpec((tm, k), lambda i: (i, 0)),
                  pl.BlockSpec((1, k), lambda i: (0, 0)),
                  pl.BlockSpec((k, 128), lambda i: (0, 0)),
                  pl.BlockSpec((1, 128), lambda i: (0, 0)),
                  pl.BlockSpec((1, 128), lambda i: (0, 0))],
        out_specs=pl.BlockSpec((tm, 128), lambda i: (i, 0)),
        out_shape=jax.ShapeDtypeStruct((t, 128), F32),
        compiler_params=_cparams("parallel"),
        name="dn_gates",
    )(x, g, w, alog_row, dtb_row)


def _mm_res_body(*refs, n_in):
    xs, ws, res_ref, o_ref = refs[:n_in], refs[n_in:2 * n_in], refs[2 * n_in], refs[2 * n_in + 1]
    acc = res_ref[...]
    for x_ref, w_ref in zip(xs, ws):
        acc = acc + _dot(x_ref[...], w_ref[...])
    o_ref[...] = acc


def _matmul_residual(xs, ws, res, tm, tn, name):
    t, n = res.shape
    n_in = len(xs)
    in_specs = ([pl.BlockSpec((tm, x.shape[1]), lambda i, j: (i, 0)) for x in xs]
                + [pl.BlockSpec((w.shape[0], tn), lambda i, j: (0, j)) for w in ws]
                + [pl.BlockSpec((tm, tn), lambda i, j: (i, j))])
    return pl.pallas_call(
        functools.partial(_mm_res_body, n_in=n_in),
        grid=(t // tm, n // tn),
        in_specs=in_specs,
        out_specs=pl.BlockSpec((tm, tn), lambda i, j: (i, j)),
        out_shape=jax.ShapeDtypeStruct((t, n), F32),
        compiler_params=_cparams("parallel", "arbitrary"),
        name=name,
    )(*xs, *ws, res)


def _ple_body(h_ref, g_ref, wg_ref, p_ref, wp_ref, fg_ref, o_ref, *, final_norm):
    h = h_ref[...]
    xn = _rms_rows(h, g_ref[...]).astype(BF16)
    gate = _sigmoid(_dot(xn, wg_ref[...]))
    h2 = h + gate * _dot(p_ref[...].astype(BF16), wp_ref[...])
    if final_norm:
        h2 = _rms_rows(h2, fg_ref[...])
    o_ref[...] = h2


def _ple(h, g, w_gate, p_all, layer, w_proj, final_g, final_norm, tm, name):
    t, d = h.shape
    pd = p_all.shape[1]
    p_block0 = layer * (t // tm)
    return pl.pallas_call(
        functools.partial(_ple_body, final_norm=final_norm),
        grid=(t // tm,),
        in_specs=[pl.BlockSpec((tm, d), lambda i: (i, 0)),
                  pl.BlockSpec((1, d), lambda i: (0, 0)),
                  pl.BlockSpec((d, d), lambda i: (0, 0)),
                  pl.BlockSpec((tm, pd), lambda i: (p_block0 + i, 0)),
                  pl.BlockSpec((pd, d), lambda i: (0, 0)),
                  pl.BlockSpec((1, d), lambda i: (0, 0))],
        out_specs=pl.BlockSpec((tm, d), lambda i: (i, 0)),
        out_shape=jax.ShapeDtypeStruct((t, d), F32),
        compiler_params=_cparams("parallel"),
        name=name,
    )(h, g, w_gate, p_all, w_proj, final_g)


def _mixer_a_body(ac_ref, ah_ref, ab_ref, az_ref, w_ref, y_ref, st_ref, carry_ref, *, emit_prod):
    @pl.when(pl.program_id(2) == 0)
    def _():
        carry_ref[...] = jnp.zeros(carry_ref.shape, F32)

    prod = ac_ref[0].astype(F32) * ah_ref[0].astype(F32)
    tl = prod.shape[0]
    row = lax.broadcasted_iota(jnp.int32, prod.shape, 0)
    c6 = carry_ref[6:7, :]
    c7 = carry_ref[7:8, :]
    p1 = jnp.where(row == 0, c7, pltpu.roll(prod, 1, axis=0))
    p2 = jnp.where(row == 0, c6, jnp.where(row == 1, c7, pltpu.roll(prod, 2, axis=0)))
    w = w_ref[...]
    u = w[2:3, :] * prod + w[1:2, :] * p1 + w[0:1, :] * p2
    y_ref[0] = (ab_ref[0].astype(F32) * u * _silu(az_ref[0].astype(F32))).astype(y_ref.dtype)
    last = prod[tl - 8:, :]
    carry_ref[...] = last
    st_ref[0] = prod if emit_prod else last


def _mixer_a(proj3, conv_w8, tl, tc, emit_prod):
    b, l, _ = proj3.shape
    nct = D_MODEL // tc
    st_rows = tl if emit_prod else 8

    def col(gi):
        return pl.BlockSpec((1, tl, tc), lambda bi, ci, li, gi=gi: (bi, li, gi * nct + ci))

    return pl.pallas_call(
        functools.partial(_mixer_a_body, emit_prod=emit_prod),
        grid=(b, nct, l // tl),
        in_specs=[col(0), col(1), col(2), col(3),
                  pl.BlockSpec((8, tc), lambda bi, ci, li: (0, ci))],
        out_specs=[pl.BlockSpec((1, tl, tc), lambda bi, ci, li: (bi, li, ci)),
                   pl.BlockSpec((1, st_rows, tc), lambda bi, ci, li: (bi, li if emit_prod else 0, ci))],
        out_shape=[jax.ShapeDtypeStruct((b, l, D_MODEL), BF16),
                   jax.ShapeDtypeStruct((b, l if emit_prod else 8, D_MODEL), F32)],
        scratch_shapes=[pltpu.VMEM((8, tc), F32)],
        compiler_params=_cparams("parallel", "parallel", "arbitrary"),
        name="mixer_a",
    )(proj3, proj3, proj3, proj3, conv_w8)


def _segment_scan(x, seg):
    n = x.shape[0]
    row = lax.broadcasted_iota(jnp.int32, x.shape, 0)
    pos = row % seg
    cs = x
    s = 1
    while s < seg:
        cs = cs + jnp.where(pos >= s, pltpu.roll(cs, s, axis=0), 0.0)
        s *= 2
    if seg == n:
        return cs, jnp.broadcast_to(cs[n - 1:n, :], x.shape)
    tot = cs
    s = 1
    while s < seg:
        tot = jnp.where(pos + s < seg, pltpu.roll(tot, n - s, axis=0), tot)
        s *= 2
    return cs, tot


def _delta_prepare(q_ref, k_ref, v_ref, gb_ref, wq_ref, wk_ref, wv_ref, r0s, prev_rows, seg):
    c = DN_CHUNK
    items = [(n, r0, j) for n, r0 in enumerate(r0s) for j in range(HEAD_GROUP)]
    ri = lax.broadcasted_iota(jnp.int32, (c, c), 0)
    ci = lax.broadcasted_iota(jnp.int32, (c, c), 1)
    bcols, gcols, glasts = [], [], []
    for r0 in r0s:
        gates = gb_ref[0, 0, pl.ds(r0, c), :]
        gsum, gtot = _segment_scan(gates, seg)
        for j in range(HEAD_GROUP):
            bcols.append(jnp.broadcast_to(gates[:, j:j + 1], (c, HEAD_DIM)))
            gcols.append(jnp.broadcast_to(gsum[:, HEAD_GROUP + j:HEAD_GROUP + j + 1], (c, HEAD_DIM)))
            glasts.append(jnp.broadcast_to(gtot[:, HEAD_GROUP + j:HEAD_GROUP + j + 1], (c, HEAD_DIM)))

    def conv_silu(which, x_ref, w_ref, n, r0, j):
        lanes = slice(j * HEAD_DIM, (j + 1) * HEAD_DIM)
        cur = x_ref[0, pl.ds(r0, c), lanes].astype(F32)
        ext = jnp.concatenate([prev_rows(which, x_ref, lanes, n), cur], axis=0)
        w = w_ref[:, lanes]
        y = (w[3:4, :] * cur + w[2:3, :] * ext[7:7 + c, :] + w[1:2, :] * ext[6:6 + c, :]
             + w[0:1, :] * ext[5:5 + c, :])
        return _silu(y)

    qs = [conv_silu(0, q_ref, wq_ref, *item) for item in items]
    ks = [conv_silu(1, k_ref, wk_ref, *item) for item in items]
    vs = [conv_silu(2, v_ref, wv_ref, *item) for item in items]
    qs = [q * lax.rsqrt(jnp.sum(q * q, axis=-1, keepdims=True) + NORM_EPS) * (HEAD_DIM ** -0.5) for q in qs]
    ks = [k * lax.rsqrt(jnp.sum(k * k, axis=-1, keepdims=True) + NORM_EPS) for k in ks]
    kbs = [k * b for k, b in zip(ks, bcols)]
    m1s = [_dot_nt(jnp.concatenate([kb, q], axis=0).astype(BF16), k.astype(BF16))
           for kb, q, k in zip(kbs, qs, ks)]
    decays = []
    for gcol in gcols:
        grow = gcol.T[:c, :]
        decay = jnp.where(ri >= ci, jnp.exp(jnp.minimum(gcol[:, :c] - grow, 0.0)), 0.0)
        if seg < c:
            decay = jnp.where(ri // seg == ci // seg, decay, 0.0)
        decays.append(decay)
    powers = [jnp.where(ri > ci, m1[:c, :] * d, 0.0) for m1, d in zip(m1s, decays)]
    a_qks = [m1[c:, :] * d for m1, d in zip(m1s, decays)]
    t_invs = [jnp.where(ri == ci, 1.0, 0.0) - a for a in powers]
    span = 2
    while span < seg:
        p16s = [p.astype(BF16) for p in powers]
        powers = [_dot(p16, p16) for p16 in p16s]
        t_invs = [t + _dot(t.astype(BF16), p.astype(BF16)) for t, p in zip(t_invs, powers)]
        span *= 2
    egs = [jnp.exp(g) for g in gcols]
    sols = [_dot(t.astype(BF16), jnp.concatenate([v * b, kb * eg], axis=1).astype(BF16))
            for t, v, b, kb, eg in zip(t_invs, vs, bcols, kbs, egs)]
    k_dec_ts = [(k * jnp.exp(gl - g)).T for k, gl, g in zip(ks, glasts, gcols)]
    flat = [(sol[:, :HEAD_DIM], sol[:, HEAD_DIM:], q * eg, a_qk, k_dec_t, gl)
            for sol, q, eg, a_qk, k_dec_t, gl in zip(sols, qs, egs, a_qks, k_dec_ts, glasts)]
    return [flat[n * HEAD_GROUP:(n + 1) * HEAD_GROUP] for n in range(len(r0s))]


def _delta_seq_body(q_ref, k_ref, v_ref, z_ref, gb_ref, wq_ref, wk_ref, wv_ref, ng_ref, y_ref, s_ref,
                    s_scr, carry, u_scr, wq_scr, ak_scr, eg_scr):
    c = DN_CHUNK
    lb = q_ref.shape[1]
    n_chunks = lb // c

    @pl.when(pl.program_id(2) == 0)
    def _():
        s_scr[...] = jnp.zeros(s_scr.shape, F32)
        carry[...] = jnp.zeros(carry.shape, F32)

    heads = range(HEAD_GROUP)

    group = 4 if n_chunks % 4 == 0 else 1

    def prepare(gi, carry_val):
        r0s = [pl.multiple_of((gi * group + n) * c, c) for n in range(group)]

        def prev_rows(which, x_ref, lanes, n):
            if n > 0:
                return x_ref[0, pl.ds(pl.multiple_of(r0s[n] - 16, 16), 16), lanes].astype(F32)[8:, :]
            p0 = pl.multiple_of(jnp.maximum(r0s[0] - 16, 0), 16)
            return jnp.where(gi == 0, carry[which, :, lanes], x_ref[0, pl.ds(p0, 16), lanes].astype(F32)[8:, :])

        parts = _delta_prepare(q_ref, k_ref, v_ref, gb_ref, wq_ref, wk_ref, wv_ref, r0s, prev_rows, c)
        for n in range(group):
            ci = gi * group + n
            for j, (u, w, qe, a_qk, k_dec_t, glast) in enumerate(parts[n]):
                u_scr[ci, j] = u
                wq_scr[ci, j] = jnp.concatenate([w, qe], axis=0).astype(BF16)
                ak_scr[ci, j] = jnp.concatenate([a_qk, k_dec_t], axis=0).astype(BF16)
                eg_scr[ci, j] = jnp.exp(glast[:8, :])
        return carry_val

    lax.fori_loop(0, n_chunks // group, prepare, 0)

    def apply(ci, carry_val):
        r0 = pl.multiple_of(ci * c, c)
        s_prevs = [s_scr[j] for j in heads]
        r1s = [_dot(wq_scr[ci, j], s_prevs[j].astype(BF16)) for j in heads]
        v_news = [u_scr[ci, j] - r1s[j][:c, :] for j in heads]
        r2s = [_dot(ak_scr[ci, j], v_news[j].astype(BF16)) for j in heads]
        for j in heads:
            lanes = slice(j * HEAD_DIM, (j + 1) * HEAD_DIM)
            s_scr[j] = s_prevs[j] * eg_scr[ci, j][0:1, :] + r2s[j][c:, :]
            o = _rms_rows(r1s[j][c:, :] + r2s[j][:c, :], ng_ref[...]) * _silu(z_ref[0, pl.ds(r0, c), lanes].astype(F32))
            y_ref[0, pl.ds(r0, c), lanes] = o.astype(y_ref.dtype)
        return carry_val

    lax.fori_loop(0, n_chunks, apply, 0)
    carry[0] = q_ref[0, lb - 16:, :].astype(F32)[8:, :]
    carry[1] = k_ref[0, lb - 16:, :].astype(F32)[8:, :]
    carry[2] = v_ref[0, lb - 16:, :].astype(F32)[8:, :]
    s_ref[0] = s_scr[...]


def _delta_seg_body(q_ref, k_ref, v_ref, z_ref, gb_ref, wq_ref, wk_ref, wv_ref, ng_ref, s0_ref, y_ref, s_ref):
    c = DN_CHUNK
    seg = SEG_ROWS
    nseg = c // seg
    lb = q_ref.shape[1]
    zeros8 = jnp.zeros((8, HEAD_DIM), F32)
    lane_seg = lax.broadcasted_iota(jnp.int32, (HEAD_DIM, c), 1) // seg
    heads = range(HEAD_GROUP)
    for ci in range(lb // c):
        r0 = ci * c
        parts = _delta_prepare(q_ref, k_ref, v_ref, gb_ref, wq_ref, wk_ref, wv_ref, [r0],
                               lambda which, x_ref, lanes, n: zeros8, seg)[0]
        r1s = []
        for j, (u, w, qe, a_qk, k_dec_t, glast) in enumerate(parts):
            for p in range(nseg):
                rows = slice(p * seg, (p + 1) * seg)
                wq_p = jnp.concatenate([w[rows, :], qe[rows, :]], axis=0).astype(BF16)
                r1s.append(_dot(wq_p, s0_ref[ci * nseg + p, j].astype(BF16)))
        vn16s, o_firsts = [], []
        for j, (u, w, qe, a_qk, k_dec_t, glast) in enumerate(parts):
            mine = r1s[j * nseg:(j + 1) * nseg]
            v_new = u - jnp.concatenate([r1[:seg, :] for r1 in mine], axis=0)
            vn16s.append(v_new.astype(BF16))
            o_firsts.append(jnp.concatenate([r1[seg:, :] for r1 in mine], axis=0))
        o_seconds = [_dot(part[3].astype(BF16), vn16) for part, vn16 in zip(parts, vn16s)]
        for j, (u, w, qe, a_qk, k_dec_t, glast) in enumerate(parts):
            lanes = slice(j * HEAD_DIM, (j + 1) * HEAD_DIM)
            eg_last = jnp.exp(glast)
            for p in range(nseg):
                upd = _dot(jnp.where(lane_seg == p, k_dec_t, 0.0).astype(BF16), vn16s[j])
                s_ref[ci * nseg + p, j] = s0_ref[ci * nseg + p, j] * eg_last[p * seg:p * seg + 1, :] + upd
            o = _rms_rows(o_firsts[j] + o_seconds[j], ng_ref[...]) * _silu(z_ref[0, pl.ds(r0, c), lanes].astype(F32))
            y_ref[0, pl.ds(r0, c), lanes] = o.astype(y_ref.dtype)


def _delta_net(proj3, gates_g, conv_w8, s0, norm_g, lb):
    b, l, _ = proj3.shape
    gw = HEAD_GROUP * HEAD_DIM
    ngrp = HEADS // HEAD_GROUP
    per_d = D_MODEL // gw
    n_chunks = lb // DN_CHUNK

    def col(gi):
        return pl.BlockSpec((1, lb, gw), lambda bi, hi, li, gi=gi: (bi, li, gi * per_d + hi))

    def wcol(gi):
        return pl.BlockSpec((8, gw), lambda bi, hi, li, gi=gi: (0, gi * per_d + hi))

    in_specs = [col(4), col(5), col(6), col(7),
                pl.BlockSpec((1, 1, lb, 128), lambda bi, hi, li: (bi, hi, li, 0)),
                wcol(0), wcol(1), wcol(2),
                pl.BlockSpec((1, HEAD_DIM), lambda bi, hi, li: (0, 0))]
    y_spec = pl.BlockSpec((1, lb, gw), lambda bi, hi, li: (bi, li, hi))
    args = [proj3, proj3, proj3, proj3, gates_g, conv_w8, conv_w8, conv_w8, norm_g]
    if s0 is None:
        state_spec = pl.BlockSpec((1, HEAD_GROUP, HEAD_DIM, HEAD_DIM), lambda bi, hi, li: (bi, hi, 0, 0))
        return pl.pallas_call(
            _delta_seq_body,
            grid=(b, ngrp, l // lb),
            in_specs=in_specs,
            out_specs=[y_spec, state_spec],
            out_shape=[jax.ShapeDtypeStruct((b, l, D_MODEL), BF16),
                       jax.ShapeDtypeStruct((b, HEADS, HEAD_DIM, HEAD_DIM), F32)],
            scratch_shapes=[pltpu.VMEM((HEAD_GROUP, HEAD_DIM, HEAD_DIM), F32),
                            pltpu.VMEM((3, 8, gw), F32),
                            pltpu.VMEM((n_chunks, HEAD_GROUP, DN_CHUNK, HEAD_DIM), F32),
                            pltpu.VMEM((n_chunks, HEAD_GROUP, 2 * DN_CHUNK, HEAD_DIM), BF16),
                            pltpu.VMEM((n_chunks, HEAD_GROUP, DN_CHUNK + HEAD_DIM, DN_CHUNK), BF16),
                            pltpu.VMEM((n_chunks, HEAD_GROUP, 8, HEAD_DIM), F32)],
            compiler_params=_cparams("parallel", "parallel", "arbitrary"),
            name="delta_net_seq",
        )(*args)
    nseq = lb // SEG_ROWS
    state_spec = pl.BlockSpec((nseq, HEAD_GROUP, HEAD_DIM, HEAD_DIM), lambda bi, hi, li: (li, hi, 0, 0))
    return pl.pallas_call(
        _delta_seg_body,
        grid=(b, ngrp, l // lb),
        in_specs=in_specs + [state_spec],
        out_specs=[y_spec, state_spec],
        out_shape=[jax.ShapeDtypeStruct((b, l, D_MODEL), BF16),
                   jax.ShapeDtypeStruct(s0.shape, F32)],
        compiler_params=_cparams("parallel", "parallel", "parallel"),
        name="delta_net_seg",
    )(*args, s0)


def _mla_qkv_body(cq_ref, ckv_ref, pair_ref, cos_ref, sin_ref, qg_ref, kvg_ref, wm_ref, ws_ref, wuk_ref, wuv_ref,
                  q_ref, ckv_out, kpe_out, kf_ref, v_ref, *, expand_kv):
    cos_t = cos_ref[...]
    sin_t = sin_ref[...]
    cqn = _rms_rows(cq_ref[...], qg_ref[...]).astype(BF16)
    ckv = _rms_rows(ckv_ref[...], kvg_ref[...])
    ckv_out[...] = ckv
    pair = pair_ref[...]
    kpe = pair[:, :128] * cos_t + pair[:, 128:] * sin_t
    kpe_out[...] = kpe
    ckv16 = ckv.astype(BF16)
    kpe16 = kpe.astype(BF16)
    for h in range(HEADS):
        qm = _dot(cqn, wm_ref[:, h * QK_PAD:(h + 1) * QK_PAD])
        qs = _dot(cqn, ws_ref[:, h * 128:(h + 1) * 128])
        q_ref[:, h * QK_PAD:h * QK_PAD + 128] = (qm[:, :128] * MLA_SCALE).astype(BF16)
        q_ref[:, h * QK_PAD + 128:(h + 1) * QK_PAD] = ((qm[:, 128:] * cos_t + qs * sin_t) * MLA_SCALE).astype(BF16)
        if expand_kv:
            kf_ref[:, h * QK_PAD:h * QK_PAD + 128] = _dot(ckv16, wuk_ref[:, h * 128:(h + 1) * 128]).astype(BF16)
            kf_ref[:, h * QK_PAD + 128:(h + 1) * QK_PAD] = kpe16
            v_ref[:, h * 128:(h + 1) * 128] = _dot(ckv16, wuv_ref[:, h * 128:(h + 1) * 128]).astype(BF16)
    if not expand_kv:
        kf_ref[...] = jnp.zeros(kf_ref.shape, kf_ref.dtype)
        v_ref[...] = jnp.zeros(v_ref.shape, v_ref.dtype)


def _mla_qkv(proj, cos_t, sin_t, qg, kvg, wm, ws, wuk, wuv, tm, expand_kv):
    t = proj.shape[0]
    kf_shape = (t, HEADS * QK_PAD) if expand_kv else (8, 128)
    v_shape = (t, HEADS * V_DIM) if expand_kv else (8, 128)
    kf_spec = (pl.BlockSpec((tm, HEADS * QK_PAD), lambda i: (i, 0)) if expand_kv
               else pl.BlockSpec((8, 128), lambda i: (0, 0)))
    v_spec = (pl.BlockSpec((tm, HEADS * V_DIM), lambda i: (i, 0)) if expand_kv
              else pl.BlockSpec((8, 128), lambda i: (0, 0)))
    const = lambda shape: pl.BlockSpec(shape, lambda i: (0, 0))
    return pl.pallas_call(
        functools.partial(_mla_qkv_body, expand_kv=expand_kv),
        grid=(t // tm,),
        in_specs=[pl.BlockSpec((tm, Q_LORA), lambda i: (i, 0)),
                  pl.BlockSpec((tm, KV_LORA), lambda i: (i, 1)),
                  pl.BlockSpec((tm, 256), lambda i: (i, PAIR_COL0 // 256)),
                  pl.BlockSpec((tm, 128), lambda i: (i, 0)),
                  pl.BlockSpec((tm, 128), lambda i: (i, 0)),
                  const((1, Q_LORA)), const((1, KV_LORA)),
                  const(wm.shape), const(ws.shape), const(wuk.shape), const(wuv.shape)],
        out_specs=[pl.BlockSpec((tm, HEADS * QK_PAD), lambda i: (i, 0)),
                   pl.BlockSpec((tm, KV_LORA), lambda i: (i, 0)),
                   pl.BlockSpec((tm, 128), lambda i: (i, 0)),
                   kf_spec, v_spec],
        out_shape=[jax.ShapeDtypeStruct((t, HEADS * QK_PAD), BF16),
                   jax.ShapeDtypeStruct((t, KV_LORA), F32),
                   jax.ShapeDtypeStruct((t, 128), F32),
                   jax.ShapeDtypeStruct(kf_shape, BF16),
                   jax.ShapeDtypeStruct(v_shape, BF16)],
        compiler_params=_cparams("arbitrary"),
        name="mla_qkv",
    )(proj, proj, proj, cos_t, sin_t, qg, kvg, wm, ws, wuk, wuv)


def _flash_body(q_ref, k_ref, v_ref, gate_ref, o_ref, *, tq, n_q, n_split):
    qi = pl.program_id(2)
    rows = tq // n_split

    def scores(q_parts, j, diagonal):
        if not diagonal:
            k = k_ref[0, j * tq:(j + 1) * tq, :]
            return [_dot_nt(qp, k) for qp in q_parts]
        return [_dot_nt(qp, k_ref[0, j * tq:j * tq + (i + 1) * rows, :]) for i, qp in enumerate(q_parts)]

    for v in range(n_q):
        @pl.when(qi == v)
        def _(v=v):
            q_parts = [q_ref[0, i * rows:(i + 1) * rows, :] for i in range(n_split)]
            m = [jnp.full((rows, 128), -jnp.inf, F32) for _ in range(n_split)]
            l = [jnp.zeros((rows, 128), F32) for _ in range(n_split)]
            acc = [jnp.zeros((rows, V_DIM), F32) for _ in range(n_split)]
            s_next = scores(q_parts, 0, v == 0)
            for j in range(v + 1):
                s_cur = s_next
                if j < v:
                    s_next = scores(q_parts, j + 1, j + 1 == v)
                for i in range(n_split):
                    s = s_cur[i]
                    n_keys = s.shape[1]
                    if j == v:
                        ri = lax.broadcasted_iota(jnp.int32, s.shape, 0) + i * rows
                        ci = lax.broadcasted_iota(jnp.int32, s.shape, 1)
                        s = jnp.where(ci <= ri, s, -jnp.inf)
                    m_new = jnp.maximum(m[i], jnp.max(s, axis=1, keepdims=True))
                    alpha = jnp.exp(m[i] - m_new)
                    e = jnp.exp(s - m_new[:, :1])
                    l[i] = alpha * l[i] + jnp.sum(e, axis=1, keepdims=True)
                    acc[i] = alpha * acc[i] + _dot(e.astype(BF16), v_ref[0, j * tq:j * tq + n_keys, :])
                    m[i] = m_new
            for i in range(n_split):
                rs = slice(i * rows, (i + 1) * rows)
                o_ref[0, rs, :] = (acc[i] / l[i] * _silu(gate_ref[0, rs, :])).astype(o_ref.dtype)


def _prompt_attention(q3, k3, v3, proj3, tq):
    b, l, _ = q3.shape
    return pl.pallas_call(
        functools.partial(_flash_body, tq=tq, n_q=l // tq, n_split=2),
        grid=(b, HEADS, l // tq),
        in_specs=[pl.BlockSpec((1, tq, QK_PAD), lambda bi, h, qi: (bi, qi, h)),
                  pl.BlockSpec((1, l, QK_PAD), lambda bi, h, qi: (bi, 0, h)),
                  pl.BlockSpec((1, l, V_DIM), lambda bi, h, qi: (bi, 0, h)),
                  pl.BlockSpec((1, tq, V_DIM), lambda bi, h, qi: (bi, qi, GATE_COL0 // V_DIM + h))],
        out_specs=pl.BlockSpec((1, tq, V_DIM), lambda bi, h, qi: (bi, qi, h)),
        out_shape=jax.ShapeDtypeStruct((b, l, HEADS * V_DIM), BF16),
        compiler_params=_cparams("parallel", "parallel", "arbitrary"),
        name="prompt_attention",
    )(q3, k3, v3, proj3)


def _head_mm_body(x_ref, w_ref, o_ref):
    o_ref[0] = _dot(x_ref[...], w_ref[0]).astype(o_ref.dtype)


def _absorb_queries(q, w_ukt):
    t = q.shape[0]
    return pl.pallas_call(
        _head_mm_body,
        grid=(HEADS,),
        in_specs=[pl.BlockSpec((t, NOPE), lambda h: (0, 2 * h)),
                  pl.BlockSpec((1, NOPE, KV_LORA), lambda h: (h, 0, 0))],
        out_specs=pl.BlockSpec((1, t, KV_LORA), lambda h: (h, 0, 0)),
        out_shape=jax.ShapeDtypeStruct((HEADS, t, KV_LORA), BF16),
        compiler_params=_cparams("parallel"),
        name="absorb_queries",
    )(q, w_ukt)


def _expand_values_body(x_ref, w_ref, gate_ref, o_ref):
    o_ref[...] = (_dot(x_ref[0], w_ref[0]) * _silu(gate_ref[...])).astype(o_ref.dtype)


def _expand_values(o_lat, w_uvh, proj):
    t = o_lat.shape[1]
    return pl.pallas_call(
        _expand_values_body,
        grid=(HEADS,),
        in_specs=[pl.BlockSpec((1, t, KV_LORA), lambda h: (h, 0, 0)),
                  pl.BlockSpec((1, KV_LORA, V_DIM), lambda h: (h, 0, 0)),
                  pl.BlockSpec((t, V_DIM), lambda h: (0, GATE_COL0 // V_DIM + h))],
        out_specs=pl.BlockSpec((t, V_DIM), lambda h: (0, h)),
        out_shape=jax.ShapeDtypeStruct((t, HEADS * V_DIM), BF16),
        compiler_params=_cparams("parallel"),
        name="expand_values",
    )(o_lat, w_uvh, proj)


def _paged_attn_body(pt_ref, ql_ref, qlt_ref, qr_ref, cn_ref, rn_ref, *rest, dec_seq):
    npg = PAGES_PER_STEP
    c_refs, r_refs = rest[:npg], rest[npg:2 * npg]
    o_ref, m_ref, l_ref, acc_ref, cbuf_even, sbuf_even, cbuf_odd, sbuf_odd = rest[2 * npg:]
    j = pl.program_id(1)
    last = pl.num_programs(1) - 1
    ql = ql_ref[0]
    qlt = qlt_ref[0]
    qr = qr_ref[0]
    rows = ql.shape[0]

    def fold(s, values):
        m_prev = m_ref[...]
        m_new = jnp.maximum(m_prev, jnp.max(s, axis=1, keepdims=True))
        alpha = jnp.exp(m_prev - m_new)
        e = jnp.exp(s - m_new[:, :1])
        l_ref[...] = alpha * l_ref[...] + jnp.sum(e, axis=1, keepdims=True)
        acc_ref[...] = alpha[:, :1] * acc_ref[...] + _dot(e.astype(BF16), values)
        m_ref[...] = m_new

    @pl.when(j == 0)
    def _():
        m_ref[...] = jnp.full(m_ref.shape, -jnp.inf, F32)
        l_ref[...] = jnp.zeros(l_ref.shape, F32)
        acc_ref[...] = jnp.zeros(acc_ref.shape, F32)
        c = cn_ref[0].astype(BF16)
        s = _dot_nt(ql, c) + _dot_nt(qr, rn_ref[0].astype(BF16))
        ri = lax.broadcasted_iota(jnp.int32, s.shape, 0)
        ci = lax.broadcasted_iota(jnp.int32, s.shape, 1)
        fold(jnp.where(ci <= ri % dec_seq, s, -jnp.inf), c)
        sbuf_odd[...] = jnp.full(sbuf_odd.shape, -jnp.inf, F32)
        cbuf_odd[...] = jnp.zeros(cbuf_odd.shape, BF16)

    for parity, (cbuf, sbuf, cbuf_prev, sbuf_prev) in enumerate(((cbuf_even, sbuf_even, cbuf_odd, sbuf_odd),
                                                                 (cbuf_odd, sbuf_odd, cbuf_even, sbuf_even))):
        @pl.when(j % 2 == parity)
        def _(cbuf=cbuf, sbuf=sbuf, cbuf_prev=cbuf_prev, sbuf_prev=sbuf_prev):
            for i, c_ref in enumerate(c_refs):
                cbuf[i * PAGE_SIZE:(i + 1) * PAGE_SIZE, :] = c_ref[0, 0].astype(BF16)
            span = SCORE_GROUP * PAGE_SIZE
            for g in range(npg // SCORE_GROUP):
                keys = slice(g * span, (g + 1) * span)
                s_lat = _dot(cbuf[keys, :], qlt).T[:rows, :]
                s_rope = jnp.concatenate([_dot(qr, r_refs[g * SCORE_GROUP + i][0, 0].astype(BF16))
                                          for i in range(SCORE_GROUP)], axis=1)
                sbuf[:, keys] = s_lat + s_rope
            fold(sbuf_prev[...], cbuf_prev[...])

            @pl.when(j == last)
            def _():
                fold(sbuf[...], cbuf[...])
                o_ref[0] = (acc_ref[...] / l_ref[...][:, :1]).astype(o_ref.dtype)


def _paged_attention(q_lat, q_r, ckv_new, kpe_new, cache_ckv, cache_kpe_t, page_table_flat, n_pages, dec_seq):
    b, rows, _ = q_lat.shape
    npg = PAGES_PER_STEP
    q_lat_t = jnp.pad(jnp.swapaxes(q_lat, 1, 2), ((0, 0), (0, 0), (0, 128 - rows)))

    def page_spec(shape, i):
        return pl.BlockSpec((1, 1) + shape, lambda bi, j, pt, i=i: (0, pt[bi * n_pages + j * npg + i], 0, 0))

    grid_spec = pltpu.PrefetchScalarGridSpec(
        num_scalar_prefetch=1,
        grid=(b, n_pages // npg),
        in_specs=([pl.BlockSpec((1, rows, KV_LORA), lambda bi, j, pt: (bi, 0, 0)),
                   pl.BlockSpec((1, KV_LORA, 128), lambda bi, j, pt: (bi, 0, 0)),
                   pl.BlockSpec((1, rows, ROPE), lambda bi, j, pt: (bi, 0, 0)),
                   pl.BlockSpec((1, PAGE_SIZE, KV_LORA), lambda bi, j, pt: (bi, 0, 0)),
                   pl.BlockSpec((1, PAGE_SIZE, ROPE), lambda bi, j, pt: (bi, 0, 0))]
                  + [page_spec((PAGE_SIZE, KV_LORA), i) for i in range(npg)]
                  + [page_spec((ROPE, PAGE_SIZE), i) for i in range(npg)]),
        out_specs=pl.BlockSpec((1, rows, KV_LORA), lambda bi, j, pt: (bi, 0, 0)),
        scratch_shapes=[pltpu.VMEM((rows, 128), F32), pltpu.VMEM((rows, 128), F32),
                        pltpu.VMEM((rows, KV_LORA), F32),
                        pltpu.VMEM((npg * PAGE_SIZE, KV_LORA), BF16),
                        pltpu.VMEM((rows, npg * PAGE_SIZE), F32),
                        pltpu.VMEM((npg * PAGE_SIZE, KV_LORA), BF16),
                        pltpu.VMEM((rows, npg * PAGE_SIZE), F32)],
    )
    return pl.pallas_call(
        functools.partial(_paged_attn_body, dec_seq=dec_seq),
        grid_spec=grid_spec,
        out_shape=jax.ShapeDtypeStruct((b, rows, KV_LORA), BF16),
        compiler_params=_cparams("parallel", "arbitrary"),
        name="paged_attention",
    )(page_table_flat, q_lat, q_lat_t, q_r, ckv_new, kpe_new, *([cache_ckv] * npg), *([cache_kpe_t] * npg))


def _prepare_weights(norm_g, ple_norm_g, w_ple_gate, w_ple_proj, final_norm_g, w_in_even, w_out_even, conv_a_w,
                     conv_qkv_w, dn_a_log, dn_dt_bias, dn_norm_g, w_in_mla, mla_q_norm_g, mla_kv_norm_g, w_uq,
                     w_uk, w_uv, w_o_mla):
    d = D_MODEL
    w = {}
    w["norm_g"] = norm_g.reshape(2, 1, d)
    w["ple_norm_g"] = ple_norm_g.reshape(2, 1, d)
    w["final_norm_g"] = final_norm_g.reshape(1, d)
    w["w_ple_gate"] = w_ple_gate.astype(BF16)
    w["w_ple_proj"] = w_ple_proj.astype(BF16)
    w_in = w_in_even[0]
    w["w_in_main"] = w_in.astype(BF16)
    w["w_in_gates"] = jnp.pad(w_in[:, 8 * d:], ((0, 0), (0, 128 - 2 * HEADS))).astype(BF16)
    w["a_log_row"] = jnp.pad(dn_a_log[0], (HEADS, 128 - 2 * HEADS)).reshape(1, 128)
    w["dt_bias_row"] = jnp.pad(dn_dt_bias[0], (HEADS, 128 - 2 * HEADS)).reshape(1, 128)
    w["w_out_a"] = w_out_even[0, :d].astype(BF16)
    w["w_out_b"] = w_out_even[0, d:].astype(BF16)
    w["conv_a_w8"] = jnp.pad(conv_a_w[0], ((0, 8 - SC_CONV), (0, 0)))
    w["conv_qkv_w8"] = jnp.pad(conv_qkv_w[0], ((0, 8 - DN_CONV), (0, 0)))
    w["dn_norm_g"] = dn_norm_g[0].reshape(1, HEAD_DIM)
    wi = w_in_mla[0]
    k_pe = wi[:, 2 * Q_LORA:2 * Q_LORA + ROPE]
    half = ROPE // 2
    zeros64 = jnp.zeros((d, 128 - ROPE), F32)
    w["w_in_mla"] = jnp.concatenate(
        [wi[:, :2 * Q_LORA], wi[:, 2 * Q_LORA + ROPE:], k_pe, zeros64, k_pe[:, half:], k_pe[:, :half], zeros64,
         jnp.zeros((d, MLA_IN_PAD - PAIR_COL0 - 256), F32)], axis=1).astype(BF16)
    w["mla_q_norm_g"] = mla_q_norm_g[0].reshape(1, Q_LORA)
    w["mla_kv_norm_g"] = mla_kv_norm_g[0].reshape(1, KV_LORA)
    uq = w_uq[0].reshape(Q_LORA, HEADS, NOPE + ROPE)
    x1, x2 = uq[:, :, NOPE:NOPE + half], uq[:, :, NOPE + half:]
    zpad = jnp.zeros((Q_LORA, HEADS, QK_PAD - NOPE - ROPE), F32)
    w["w_uq_main"] = jnp.concatenate([uq, zpad], axis=2).reshape(Q_LORA, HEADS * QK_PAD).astype(BF16)
    w["w_uq_swap"] = jnp.concatenate([x2, x1, zpad], axis=2).reshape(Q_LORA, HEADS * 128).astype(BF16)
    w["w_uk"] = w_uk[0].reshape(KV_LORA, HEADS * NOPE).astype(BF16)
    w["w_uv"] = w_uv[0].reshape(KV_LORA, HEADS * V_DIM).astype(BF16)
    w["w_uk_t"] = jnp.transpose(w_uk[0], (1, 2, 0)).astype(BF16)
    w["w_uv_h"] = jnp.transpose(w_uv[0], (1, 0, 2)).astype(BF16)
    w["w_o_mla"] = w_o_mla[0].astype(BF16)
    return w


def _rope_tables(pos):
    half = ROPE // 2
    inv = ROPE_THETA ** (-jnp.arange(half, dtype=F32) / half)
    ang = pos.astype(F32)[:, None] * inv
    cos, sin = jnp.cos(ang), jnp.sin(ang)
    z = jnp.zeros((pos.shape[0], 128 - ROPE), F32)
    return jnp.concatenate([cos, cos, z], axis=1), jnp.concatenate([-sin, sin, z], axis=1)


def _group_gates(gates3):
    b, l, _ = gates3.shape
    ngrp = HEADS // HEAD_GROUP
    beta = gates3[:, :, :HEADS].reshape(b, l, ngrp, HEAD_GROUP)
    g = gates3[:, :, HEADS:2 * HEADS].reshape(b, l, ngrp, HEAD_GROUP)
    grouped = jnp.transpose(jnp.concatenate([beta, g], axis=-1), (0, 2, 1, 3))
    return jnp.pad(grouped, ((0, 0), (0, 0), (0, 0), (0, 128 - 2 * HEAD_GROUP)))


def _layer_stack(x, p, pos, conv_a0, conv_qkv0, delta0, past, w, tm):
    b, l, d = x.shape
    t = b * l
    h = x.reshape(t, d)

    proj = _rms_matmul(h, w["norm_g"][0], w["w_in_main"], BF16 if past is None else F32, tm, 2048, "in_proj_even",
                       n=8 * d)
    gates = _rms_matmul_gates(h, w["norm_g"][0], w["w_in_gates"], w["a_log_row"], w["dt_bias_row"], tm)
    proj3 = proj.reshape(b, l, 8 * d)
    gates3 = gates.reshape(b, l, 128)
    conv_qkv1 = proj3[:, l - (DN_CONV - 1):l, 4 * d:7 * d].astype(F32)
    if past is None:
        y_a, a_last8 = _mixer_a(proj3, w["conv_a_w8"], 512, 512, False)
        y_b, delta1 = _delta_net(proj3, _group_gates(gates3), w["conv_qkv_w8"], None, w["dn_norm_g"], 256)
        conv_a1 = a_last8[:, 8 - (SC_CONV - 1):]
    else:
        n_state = DN_CONV - 1
        tail = SEG_ROWS - n_state - l
        a_state = jnp.pad(conv_a0, ((0, 0), (n_state - (SC_CONV - 1), 0), (0, 0)))
        a_ones = jnp.pad(jnp.ones((b, SC_CONV - 1, d), F32), ((0, 0), (n_state - (SC_CONV - 1), 0), (0, 0)))
        state_rows = jnp.concatenate([a_state, a_ones, jnp.zeros((b, n_state, 2 * d), F32), conv_qkv0,
                                      jnp.zeros((b, n_state, d), F32)], axis=2)
        stacked = jnp.concatenate([state_rows, proj3, jnp.zeros((b, tail, 8 * d), F32)], axis=1)
        stacked = stacked.reshape(1, b * SEG_ROWS, 8 * d)
        gates_s = jnp.pad(gates3, ((0, 0), (n_state, tail), (0, 0))).reshape(1, b * SEG_ROWS, 128)
        tl = min(b * SEG_ROWS, 512)
        y_a, prod = _mixer_a(stacked, w["conv_a_w8"], tl, 512, True)
        y_b, delta1 = _delta_net(stacked, _group_gates(gates_s), w["conv_qkv_w8"], delta0, w["dn_norm_g"], DN_CHUNK)
        tok = slice(n_state, n_state + l)
        y_a = y_a.reshape(b, SEG_ROWS, d)[:, tok]
        y_b = y_b.reshape(b, SEG_ROWS, d)[:, tok]
        conv_a1 = prod.reshape(b, SEG_ROWS, d)[:, n_state + l - (SC_CONV - 1):n_state + l]
    h = _matmul_residual([y_a.reshape(t, d), y_b.reshape(t, d)], [w["w_out_a"], w["w_out_b"]], h, min(tm, 512),
                         1024, "out_proj_even")
    p_all = p.reshape(p.shape[0] * t, PLE_DIM)
    h = _ple(h, w["ple_norm_g"][0], w["w_ple_gate"][0], p_all, 0, w["w_ple_proj"][0],
             w["final_norm_g"], False, min(tm, 512), "ple0")

    proj = _rms_matmul(h, w["norm_g"][1], w["w_in_mla"], F32, tm, 512, "in_proj_mla")
    cos_t, sin_t = _rope_tables(pos)
    cos_t = jnp.broadcast_to(cos_t[None], (b, l, 128)).reshape(t, 128)
    sin_t = jnp.broadcast_to(sin_t[None], (b, l, 128)).reshape(t, 128)
    q, ckv, kpe128, kfull, v = _mla_qkv(proj, cos_t, sin_t, w["mla_q_norm_g"], w["mla_kv_norm_g"], w["w_uq_main"],
                                        w["w_uq_swap"], w["w_uk"], w["w_uv"], min(t, 256), past is None)
    kpe = kpe128[:, :ROPE]
    if past is None:
        o = _prompt_attention(q.reshape(b, l, HEADS * QK_PAD), kfull.reshape(b, l, HEADS * QK_PAD),
                              v.reshape(b, l, HEADS * V_DIM), proj.reshape(b, l, MLA_IN_PAD), 512)
        o = o.reshape(t, HEADS * V_DIM)
    else:
        cache_ckv, cache_kpe, page_table = past
        n_pages = page_table.shape[1]
        q_lat = _absorb_queries(q, w["w_uk_t"])
        q_lat = jnp.transpose(q_lat.reshape(HEADS, b, l, KV_LORA), (1, 0, 2, 3)).reshape(b, HEADS * l, KV_LORA)
        q_r = q.reshape(b, l, HEADS, QK_PAD)[:, :, :, NOPE:NOPE + ROPE]
        q_r = jnp.transpose(q_r, (0, 2, 1, 3)).reshape(b, HEADS * l, ROPE)
        ckv_new = jnp.pad(ckv.reshape(b, l, KV_LORA), ((0, 0), (0, PAGE_SIZE - l), (0, 0)))
        kpe_new = jnp.pad(kpe.reshape(b, l, ROPE), ((0, 0), (0, PAGE_SIZE - l), (0, 0)))
        o_lat = _paged_attention(q_lat, q_r, ckv_new, kpe_new, cache_ckv, jnp.swapaxes(cache_kpe, 2, 3),
                                 page_table.reshape(-1), n_pages, l)
        o_lat = jnp.transpose(o_lat.reshape(b, HEADS, l, KV_LORA), (1, 0, 2, 3)).reshape(HEADS, t, KV_LORA)
        o = _expand_values(o_lat, w["w_uv_h"], proj)
    h = _matmul_residual([o], [w["w_o_mla"]], h, min(tm, 512), 1024, "out_proj_mla")
    y = _ple(h, w["ple_norm_g"][1], w["w_ple_gate"][1], p_all, 1, w["w_ple_proj"][1],
             w["final_norm_g"], True, min(tm, 512), "ple1")
    return (y.reshape(b, l, d), conv_a1[None], conv_qkv1[None], delta1[None],
            ckv.reshape(1, b, l, KV_LORA), kpe.reshape(1, b, l, ROPE))


def kernel(x_prompt, x_sample, cache_ckv, cache_kpe, state_conv_a, state_conv_qkv, state_delta, page_table,
           p_prompt, p_sample, norm_g, ple_norm_g, w_ple_gate, w_ple_proj, final_norm_g, w_in_even, w_out_even,
           conv_a_w, conv_qkv_w, dn_a_log, dn_dt_bias, dn_norm_g, w_in_mla, mla_q_norm_g, mla_kv_norm_g, w_uq,
           w_uk, w_uv, w_o_mla):
    w = _prepare_weights(norm_g, ple_norm_g, w_ple_gate, w_ple_proj, final_norm_g, w_in_even, w_out_even, conv_a_w,
                         conv_qkv_w, dn_a_log, dn_dt_bias, dn_norm_g, w_in_mla, mla_q_norm_g, mla_kv_norm_g, w_uq,
                         w_uk, w_uv, w_o_mla)
    seq = x_prompt.shape[1]
    dec_b, dec_l, _ = x_sample.shape
    past_len = page_table.shape[1] * PAGE_SIZE
    outs_p = _layer_stack(x_prompt, p_prompt, jnp.arange(seq), None, None, None, None, w, 1024)
    outs_s = _layer_stack(x_sample, p_sample, past_len + jnp.arange(dec_l), state_conv_a[0], state_conv_qkv[0],
                          state_delta[0], (cache_ckv, cache_kpe, page_table), w, dec_b * dec_l)
    return (outs_p[0], outs_s[0]) + tuple(outs_p[1:]) + tuple(outs_s[1:])
```
